```python
import math
import jax, jax.numpy as jnp
from jax import lax
import numpy as np

D_MODEL = 1024
BATCH = 8
SEQ = 8192
DEPTH = 2

MIX_WIDTH = D_MODEL
HEAD_DIM = 64
ATT_WIDTH = MIX_WIDTH // 2
N_ATT_HEADS = ATT_WIDTH // HEAD_DIM
N_KV_HEADS = 2
ATT_GQA = N_ATT_HEADS // N_KV_HEADS
KV_WIDTH = N_KV_HEADS * HEAD_DIM
N_BRANCH = 3
CMP_BLOCK = 32
CMP_STRIDE = 16
CMP_HIDDEN = 256
SEL_BLOCK = 64
SEL_TOPN = 16
SEL_FORCE_BONUS = 1e4
WINDOW = 512
Q_BLOCK = 64
SSM_WIDTH = MIX_WIDTH - ATT_WIDTH
SSM_HEAD_DIM = 64
N_SSM_HEADS = SSM_WIDTH // SSM_HEAD_DIM
SSM_GROUPS = 2
SSM_STATE = 128
SSM_CONV = 4
SSM_CHUNK = 128
SSM_CONV_DIM = SSM_WIDTH + 2 * SSM_GROUPS * SSM_STATE
FFN_DENSE = 2816
N_EXPERTS = 8
TOP_K = 2
FFN_EXPERT = 3584
MOE_BLOCK = 512
N_DENSE_LAYERS = (DEPTH + 1) // 2
N_MOE_LAYERS = DEPTH // 2
DN_ALPHA = (2 * DEPTH) ** 0.25
DN_BETA = (8 * DEPTH) ** -0.25
LN_EPS = 1e-5
RMS_EPS = 1e-6
NEG = -1e30
IN_SIZES = (ATT_WIDTH, KV_WIDTH, KV_WIDTH, KV_WIDTH, KV_WIDTH, KV_WIDTH, KV_WIDTH, N_ATT_HEADS * N_BRANCH, SSM_WIDTH, SSM_CONV_DIM, N_SSM_HEADS)
IN_WIDTH = sum(IN_SIZES)

kernel_name = 'hybrid_nsa_ssd_moe_deepnorm_block'


def layer_norm(x, w=None, b=None):
    xf = x.astype(jnp.float32)
    mu = jnp.mean(xf, -1, keepdims=True)
    var = jnp.mean(jnp.square(xf - mu), -1, keepdims=True)
    y = (xf - mu) * lax.rsqrt(var + LN_EPS)
    if w is not None:
        y = y * w.astype(jnp.float32) + b.astype(jnp.float32)
    return y.astype(x.dtype)


def rms_norm(x, w):
    xf = x.astype(jnp.float32)
    y = xf * lax.rsqrt(jnp.mean(jnp.square(xf), -1, keepdims=True) + RMS_EPS)
    return (y * w.astype(jnp.float32)).astype(x.dtype)


def masked_softmax(s, mask):
    s = jnp.where(mask, s.astype(jnp.float32), NEG)
    m = jnp.max(s, -1, keepdims=True)
    e = jnp.where(mask, jnp.exp(s - m), 0.0)
    return e / jnp.maximum(jnp.sum(e, -1, keepdims=True), 1e-30)


def compress_blocks(kv, pos, w1, w2):
    b, s, g, d = kv.shape
    n_chunk = s // CMP_STRIDE
    r = CMP_BLOCK // CMP_STRIDE
    n_cmp = n_chunk - r + 1
    ch = kv.reshape(b, n_chunk, CMP_STRIDE, g, d)
    blocks = jnp.concatenate([ch[:, j:j + n_cmp] for j in range(r)], axis=2)
    blocks = blocks + pos[None, None, :, None, :]
    flat = blocks.transpose(0, 1, 3, 2, 4).reshape(b, n_cmp, g, CMP_BLOCK * d)
    return jax.nn.gelu(flat @ w1) @ w2


def nsa_attention(q, kc, vc, ks, vs, kw, vw, gates):
    b, s = q.shape[:2]
    g, r, d = N_KV_HEADS, ATT_GQA, HEAD_DIM
    n_cmp = kc.shape[1]
    n_sel = s // SEL_BLOCK
    top_n = min(SEL_TOPN, n_sel)
    qg = (q * (d ** -0.5)).reshape(b, s, g, r, d)
    gates = gates.reshape(b, s, g, r, N_BRANCH)
    cmp_start = jnp.arange(n_cmp) * CMP_STRIDE
    cmp_last = cmp_start + CMP_BLOCK - 1
    sel_start = jnp.arange(n_sel) * SEL_BLOCK
    overlap = ((cmp_start[:, None] < sel_start[None, :] + SEL_BLOCK) & (cmp_start[:, None] + CMP_BLOCK > sel_start[None, :])).astype(jnp.float32)
    ks_blk = ks.reshape(b, n_sel, SEL_BLOCK, g, d).transpose(0, 3, 1, 2, 4)
    vs_blk = vs.reshape(b, n_sel, SEL_BLOCK, g, d).transpose(0, 3, 1, 2, 4)
    kw_pad = jnp.pad(kw, ((0, 0), (WINDOW, 0), (0, 0), (0, 0)))
    vw_pad = jnp.pad(vw, ((0, 0), (WINDOW, 0), (0, 0), (0, 0)))
    sel_j = jnp.arange(n_sel)
    blk_off = jnp.arange(SEL_BLOCK)
    win_off = jnp.arange(WINDOW + Q_BLOCK)
    gather_blocks = jax.vmap(jax.vmap(lambda kb, ix: kb[ix]))

    def query_block(i):
        s0 = i * Q_BLOCK
        t = s0 + jnp.arange(Q_BLOCK)
        qb = lax.dynamic_slice_in_dim(qg, s0, Q_BLOCK, axis=1)
        gb = lax.dynamic_slice_in_dim(gates, s0, Q_BLOCK, axis=1)
        p_cmp = masked_softmax(jnp.einsum('bqgrd,bcgd->bgrqc', qb, kc), cmp_last[None, :] <= t[:, None])
        o_cmp = jnp.einsum('bgrqc,bcgd->bqgrd', p_cmp.astype(vc.dtype), vc)
        imp = jnp.einsum('bgrqc,cj->bgqj', p_cmp, overlap)
        jt = (t // SEL_BLOCK)[:, None]
        forced = (sel_j == 0) | (sel_j == jt) | (sel_j == jt - 1)
        score = jnp.where(sel_j <= jt, imp + jnp.where(forced, SEL_FORCE_BONUS, 0.0), NEG)
        _, idx = lax.top_k(score, top_n)
        k_sel = gather_blocks(ks_blk, idx).reshape(b, g, Q_BLOCK, top_n * SEL_BLOCK, d)
        v_sel = gather_blocks(vs_blk, idx).reshape(b, g, Q_BLOCK, top_n * SEL_BLOCK, d)
        k_pos = (idx[..., None] * SEL_BLOCK + blk_off).reshape(b, g, Q_BLOCK, top_n * SEL_BLOCK)
        p_sel = masked_softmax(jnp.einsum('bqgrd,bgqkd->bgrqk', qb, k_sel), (k_pos <= t[:, None])[:, :, None])
        o_sel = jnp.einsum('bgrqk,bgqkd->bqgrd', p_sel.astype(v_sel.dtype), v_sel)
        k_win = lax.dynamic_slice_in_dim(kw_pad, s0, WINDOW + Q_BLOCK, axis=1)
        v_win = lax.dynamic_slice_in_dim(vw_pad, s0, WINDOW + Q_BLOCK, axis=1)
        w_pos = s0 - WINDOW + win_off
        w_mask = (w_pos[None, :] <= t[:, None]) & (w_pos[None, :] > t[:, None] - WINDOW) & (w_pos[None, :] >= 0)
        p_win = masked_softmax(jnp.einsum('bqgrd,bkgd->bgrqk', qb, k_win), w_mask)
        o_win = jnp.einsum('bgrqk,bkgd->bqgrd', p_win.astype(v_win.dtype), v_win)
        out = gb[..., 0:1] * o_cmp + gb[..., 1:2] * o_sel + gb[..., 2:3] * o_win
        return out.reshape(b, Q_BLOCK, g * r * d)

    out = lax.map(query_block, jnp.arange(s // Q_BLOCK))
    return out.swapaxes(0, 1).reshape(b, s, g * r * d)


def causal_conv(x, w, bias):
    y = lax.conv_general_dilated(x, w[:, None, :], window_strides=(1,), padding=[(SSM_CONV - 1, 0)],
                                 dimension_numbers=('NWC', 'WIO', 'NWC'), feature_group_count=x.shape[-1])
    return y + bias


def ssd_chunked(x, dt, a, bm, cm, d_skip):
    b, s, h, p = x.shape
    n_chunk = s // SSM_CHUNK
    hpg = h // bm.shape[2]

    def chunks(t):
        return t.astype(jnp.float32).reshape(b, n_chunk, SSM_CHUNK, *t.shape[2:]).swapaxes(0, 1)

    causal = jnp.tril(jnp.ones((SSM_CHUNK, SSM_CHUNK), dtype=bool))[None, :, :, None]

    def step(state, inp):
        xc, dtc, bc, cc = inp
        bh = jnp.repeat(bc, hpg, axis=2)
        chh = jnp.repeat(cc, hpg, axis=2)
        acum = jnp.cumsum(dtc * a, axis=1)
        decay_ij = jnp.exp(jnp.where(causal, acum[:, :, None, :] - acum[:, None, :, :], NEG))
        xdt = xc * dtc[..., None]
        scores = jnp.einsum('bihn,bjhn->bijh', chh, bh) * decay_ij
        y = jnp.einsum('bijh,bjhp->bihp', scores, xdt)
        y = y + jnp.einsum('bihn,bhpn->bihp', chh * jnp.exp(acum)[..., None], state)
        last = acum[:, -1]
        to_end = jnp.exp(last[:, None, :] - acum)
        state = jnp.exp(last)[:, :, None, None] * state + jnp.einsum('bjhn,bjhp->bhpn', bh * to_end[..., None], xdt)
        return state, y + d_skip[:, None] * xc

    state0 = jnp.zeros((b, h, p, bm.shape[-1]), jnp.float32)
    _, ys = lax.scan(step, state0, (chunks(x), chunks(dt), chunks(bm), chunks(cm)))
    return ys.swapaxes(0, 1).reshape(b, s, h * p)


def mixer(u, w_in, cmp_pos_k, cmp_w1_k, cmp_w2_k, cmp_pos_v, cmp_w1_v, cmp_w2_v, attn_norm_w,
          conv_w, conv_b, dt_bias, a_log, d_skip, ssm_norm_w, w_out):
    b, s, _ = u.shape
    offsets = np.cumsum(IN_SIZES)[:-1].tolist()
    q, kc_raw, vc_raw, ks, vs, kw, vw, g_raw, z, xbc, dt_raw = jnp.split(u @ w_in, offsets, axis=-1)
    kvh = lambda t: t.reshape(b, s, N_KV_HEADS, HEAD_DIM)
    kc = compress_blocks(kvh(kc_raw), cmp_pos_k, cmp_w1_k, cmp_w2_k)
    vc = compress_blocks(kvh(vc_raw), cmp_pos_v, cmp_w1_v, cmp_w2_v)
    att = nsa_attention(q.reshape(b, s, N_ATT_HEADS, HEAD_DIM), kc, vc, kvh(ks), kvh(vs), kvh(kw), kvh(vw),
                        jax.nn.sigmoid(g_raw))
    att = rms_norm(att, attn_norm_w)
    xbc = jax.nn.silu(causal_conv(xbc, conv_w, conv_b))
    xs, bm, cm = jnp.split(xbc, [SSM_WIDTH, SSM_WIDTH + SSM_GROUPS * SSM_STATE], axis=-1)
    dt = jax.nn.softplus((dt_raw + dt_bias).astype(jnp.float32))
    a = -jnp.exp(a_log.astype(jnp.float32))
    y = ssd_chunked(xs.reshape(b, s, N_SSM_HEADS, SSM_HEAD_DIM), dt, a,
                    bm.reshape(b, s, SSM_GROUPS, SSM_STATE), cm.reshape(b, s, SSM_GROUPS, SSM_STATE),
                    d_skip.astype(jnp.float32)).astype(u.dtype)
    yz = (y * jax.nn.silu(z)).reshape(b, s, SSM_GROUPS, SSM_WIDTH // SSM_GROUPS)
    ssm = rms_norm(yz, ssm_norm_w.reshape(SSM_GROUPS, -1)).reshape(b, s, SSM_WIDTH)
    return jnp.concatenate([att, ssm], axis=-1) @ w_out


def swiglu(u, wg, wu, wd):
    return (jax.nn.silu(u @ wg) * (u @ wu)) @ wd


def moe_swiglu(u, router_w, wg, wu, wd):
    n_tok, dm = u.shape
    logits = (u @ router_w).astype(jnp.float32)
    top_logit, top_e = lax.top_k(logits, TOP_K)
    gate = jax.nn.softmax(top_logit, axis=-1).astype(u.dtype)
    n_pair = n_tok * TOP_K
    e_flat = top_e.reshape(n_pair)
    tok_flat = jnp.repeat(jnp.arange(n_tok, dtype=jnp.int32), TOP_K)
    order = jnp.argsort(e_flat)
    e_sorted = e_flat[order]
    counts = jnp.bincount(e_flat, length=N_EXPERTS)
    padded = (counts + MOE_BLOCK - 1) // MOE_BLOCK * MOE_BLOCK
    grp_start = jnp.cumsum(counts) - counts
    pad_end = jnp.cumsum(padded)
    pad_start = pad_end - padded
    dest = pad_start[e_sorted] + jnp.arange(n_pair) - grp_start[e_sorted]
    n_blk = -(-n_pair // MOE_BLOCK) + N_EXPERTS
    n_row = n_blk * MOE_BLOCK
    row_tok = jnp.zeros((n_row,), jnp.int32).at[dest].set(tok_flat[order])
    row_gate = jnp.zeros((n_row,), u.dtype).at[dest].set(gate.reshape(n_pair)[order])
    blk_expert = jnp.minimum(jnp.searchsorted(pad_end, jnp.arange(n_blk) * MOE_BLOCK, side='right'), N_EXPERTS - 1)

    def expert_block(args):
        bi, e = args
        xb = u[lax.dynamic_slice_in_dim(row_tok, bi * MOE_BLOCK, MOE_BLOCK)]
        return (jax.nn.silu(xb @ wg[e]) * (xb @ wu[e])) @ wd[e]

    y_rows = lax.map(expert_block, (jnp.arange(n_blk), blk_expert)).reshape(n_row, dm)
    return jnp.zeros_like(u).at[row_tok].add(y_rows * row_gate[:, None])


def setup_inputs(seed: int = 0) -> dict:
    key = jax.random.key(seed)
    keys = iter(jax.random.split(key, 40))
    nrm = lambda shape, scale: jax.random.normal(next(keys), shape, jnp.float32) * scale
    L = DEPTH
    dt0 = jnp.exp(jax.random.uniform(next(keys), (L, N_SSM_HEADS), jnp.float32, math.log(1e-3), math.log(1e-1)))
    return {
        'x': nrm((BATCH, SEQ, D_MODEL), 1.0),
        'c': nrm((BATCH, D_MODEL), 1.0),
        'ada_w': nrm((L, D_MODEL, 6 * D_MODEL), 0.1 * D_MODEL ** -0.5),
        'ada_b': nrm((L, 6 * D_MODEL), 0.01),
        'w_in': nrm((L, D_MODEL, IN_WIDTH), D_MODEL ** -0.5),
        'cmp_pos_k': nrm((L, CMP_BLOCK, HEAD_DIM), 0.02),
        'cmp_w1_k': nrm((L, CMP_BLOCK * HEAD_DIM, CMP_HIDDEN), (CMP_BLOCK * HEAD_DIM) ** -0.5),
        'cmp_w2_k': nrm((L, CMP_HIDDEN, HEAD_DIM), CMP_HIDDEN ** -0.5),
        'cmp_pos_v': nrm((L, CMP_BLOCK, HEAD_DIM), 0.02),
        'cmp_w1_v': nrm((L, CMP_BLOCK * HEAD_DIM, CMP_HIDDEN), (CMP_BLOCK * HEAD_DIM) ** -0.5),
        'cmp_w2_v': nrm((L, CMP_HIDDEN, HEAD_DIM), CMP_HIDDEN ** -0.5),
        'attn_norm_w': 1.0 + nrm((L, ATT_WIDTH), 0.01),
        'conv_w': nrm((L, SSM_CONV, SSM_CONV_DIM), SSM_CONV ** -0.5),
        'conv_b': nrm((L, SSM_CONV_DIM), 0.01),
        'dt_bias': dt0 + jnp.log(-jnp.expm1(-dt0)),
        'a_log': jnp.log(jax.random.uniform(next(keys), (L, N_SSM_HEADS), jnp.float32, 1.0, 16.0)),
        'd_skip': 1.0 + nrm((L, N_SSM_HEADS), 0.01),
        'ssm_norm_w': 1.0 + nrm((L, SSM_WIDTH), 0.01),
        'w_out': nrm((L, MIX_WIDTH, D_MODEL), DN_BETA * MIX_WIDTH ** -0.5),
        'ln1_w': 1.0 + nrm((L, D_MODEL), 0.01),
        'ln1_b': nrm((L, D_MODEL), 0.01),
        'ln2_w': 1.0 + nrm((L, D_MODEL), 0.01),
        'ln2_b': nrm((L, D_MODEL), 0.01),
        'ffn_w_gate': nrm((N_DENSE_LAYERS, D_MODEL, FFN_DENSE), D_MODEL ** -0.5),
        'ffn_w_up': nrm((N_DENSE_LAYERS, D_MODEL, FFN_DENSE), D_MODEL ** -0.5),
        'ffn_w_down': nrm((N_DENSE_LAYERS, FFN_DENSE, D_MODEL), DN_BETA * FFN_DENSE ** -0.5),
        'router_w': nrm((N_MOE_LAYERS, D_MODEL, N_EXPERTS), D_MODEL ** -0.5),
        'exp_w_gate': nrm((N_MOE_LAYERS, N_EXPERTS, D_MODEL, FFN_EXPERT), D_MODEL ** -0.5),
        'exp_w_up': nrm((N_MOE_LAYERS, N_EXPERTS, D_MODEL, FFN_EXPERT), D_MODEL ** -0.5),
        'exp_w_down': nrm((N_MOE_LAYERS, N_EXPERTS, FFN_EXPERT, D_MODEL), DN_BETA * FFN_EXPERT ** -0.5),
    }


def reference(x, c, ada_w, ada_b, w_in, cmp_pos_k, cmp_w1_k, cmp_w2_k, cmp_pos_v, cmp_w1_v, cmp_w2_v,
              attn_norm_w, conv_w, conv_b, dt_bias, a_log, d_skip, ssm_norm_w, w_out,
              ln1_w, ln1_b, ln2_w, ln2_b, ffn_w_gate, ffn_w_up, ffn_w_down,
              router_w, exp_w_gate, exp_w_up, exp_w_down):
    cond = jax.nn.silu(c)
    for layer in range(DEPTH):
        mod = cond @ ada_w[layer] + ada_b[layer]
        shift1, scale1, gate1, shift2, scale2, gate2 = jnp.split(mod[:, None, :], 6, axis=-1)
        u = layer_norm(x) * (1 + scale1) + shift1
        h = mixer(u, w_in[layer], cmp_pos_k[layer], cmp_w1_k[layer], cmp_w2_k[layer],
                  cmp_pos_v[layer], cmp_w1_v[layer], cmp_w2_v[layer], attn_norm_w[layer],
                  conv_w[layer], conv_b[layer], dt_bias[layer], a_log[layer], d_skip[layer],
                  ssm_norm_w[layer], w_out[layer])
        x = layer_norm(DN_ALPHA * x + (1 + gate1) * h, ln1_w[layer], ln1_b[layer])
        u = layer_norm(x) * (1 + scale2) + shift2
        i = layer // 2
        if layer % 2 == 0:
            f = swiglu(u, ffn_w_gate[i], ffn_w_up[i], ffn_w_down[i])
        else:
            f = moe_swiglu(u.reshape(-1, D_MODEL), router_w[i], exp_w_gate[i], exp_w_up[i], exp_w_down[i]).reshape(x.shape)
        x = layer_norm(DN_ALPHA * x + (1 + gate2) * f, ln2_w[layer], ln2_b[layer])
    return x
```

```python
import functools
import math

import numpy as np
import jax
import jax.numpy as jnp
from jax import lax
from jax.experimental import pallas as pl
from jax.experimental.pallas import tpu as pltpu

F32 = jnp.float32
BF16 = jnp.bfloat16
I32 = jnp.int32

D_MODEL = 1024
HEAD_DIM = 64
ATT_WIDTH = 512
N_KV_HEADS = 2
ATT_GQA = 4
CMP_BLOCK = 32
CMP_STRIDE = 16
CMP_HIDDEN = 256
SEL_BLOCK = 64
SEL_TOPN = 16
SEL_FORCE_BONUS = 1e4
WINDOW = 512
SSM_WIDTH = 512
N_SSM_HEADS = 8
SSM_STATE = 128
SSM_CONV = 4
SSM_CHUNK = 128
FFN_DENSE = 2816
N_EXPERTS = 8
FFN_EXPERT = 3584
DEPTH = 2
DN_ALPHA = (2 * DEPTH) ** 0.25
LN_EPS = 1e-5
RMS_EPS = 1e-6
NEG = -1e30

LANE = 128
VMEM_LIMIT = 56 * 1024 * 1024

TM = 512
TQ = 128
TK = 512
MOE_TILE = 512
MOE_ALIGN = 16
MOE_BLK = 1024
FCHUNK = 512


def _cparams(sem):
    return pltpu.CompilerParams(dimension_semantics=sem, vmem_limit_bytes=VMEM_LIMIT)


def _dot(a, b):
    return jnp.dot(a, b, preferred_element_type=F32)


def _dot_nt(a, b):
    return lax.dot_general(a, b, (((1,), (1,)), ((), ())), preferred_element_type=F32)


def _split2(x):
    hi = x.astype(BF16)
    lo = (x - hi.astype(F32)).astype(BF16)
    return hi, lo


def _split3(x):
    hi = x.astype(BF16)
    r = x - hi.astype(F32)
    mid = r.astype(BF16)
    lo = (r - mid.astype(F32)).astype(BF16)
    return hi, mid, lo


def _sigmoid(x):
    return 1.0 / (1.0 + jnp.exp(-x))


def _silu(x):
    return x * _sigmoid(x)


def _ln_rows(x):
    mu = jnp.mean(x, axis=-1, keepdims=True)
    xc = x - mu
    var = jnp.mean(xc * xc, axis=-1, keepdims=True)
    return xc * lax.rsqrt(var + LN_EPS)


def _masked_softmax(s, mask):
    s = jnp.where(mask, s, NEG)
    m = jnp.max(s, axis=-1, keepdims=True)
    e = jnp.where(mask, jnp.exp(s - m), 0.0)
    return e / jnp.maximum(jnp.sum(e, axis=-1, keepdims=True), 1e-30)


def _ada_kernel(c_ref, w_ref, b_ref, o_ref):
    cond = _silu(c_ref[...])
    ch, cl = _split2(cond)
    wh, wl = _split2(w_ref[0])
    o_ref[0] = _dot(ch, wh) + _dot(cl, wh) + _dot(ch, wl) + b_ref[0]


def ada_mod(c, ada_w, ada_b):
    nl, d, n = ada_w.shape
    b = c.shape[0]
    tn = 1536
    return pl.pallas_call(
        _ada_kernel,
        grid=(nl, n // tn),
        in_specs=[
            pl.BlockSpec((b, d), lambda l, j: (0, 0)),
            pl.BlockSpec((1, d, tn), lambda l, j: (l, 0, j)),
            pl.BlockSpec((1, 1, tn), lambda l, j: (l, 0, j)),
        ],
        out_specs=pl.BlockSpec((1, b, tn), lambda l, j: (l, 0, j)),
        out_shape=jax.ShapeDtypeStruct((nl, b, n), F32),
        compiler_params=_cparams(("parallel", "parallel")),
        name="ada_mod",
    )(c, ada_w, ada_b.reshape(nl, 1, n))


KV_COLS = 6 * 128
GATE_COLS = 24
DT_LANE0 = 32


def _inproj_kernel(x_ref, mod_ref, wq_ref, wkv_ref, wz_ref, wx_ref, wsh_ref, wsl_ref, dtb_ref,
                   q_ref, kv_ref, z_ref, xbc_ref, sm_ref):
    xn = _ln_rows(x_ref[...])
    shift = mod_ref[0, 0:1, :]
    scale = mod_ref[0, 1:2, :]
    u = xn * (1.0 + scale) + shift
    ub = u.astype(BF16)
    ul = (u - ub.astype(F32)).astype(BF16)
    q_ref[...] = (_dot(ub, wq_ref[...]) * (HEAD_DIM ** -0.5)).astype(BF16)
    kv_ref[...] = _dot(ub, wkv_ref[...]).astype(BF16)
    z_ref[...] = _dot(ub, wz_ref[...])
    xbc_ref[...] = _dot(ub, wx_ref[...])
    sm = _dot(ub, wsh_ref[...]) + _dot(ul, wsh_ref[...]) + _dot(ub, wsl_ref[...])
    lane = lax.broadcasted_iota(I32, sm.shape, 1)
    xdt = sm + dtb_ref[...]
    dt = jnp.maximum(xdt, 0.0) + jnp.log(1.0 + jnp.exp(-jnp.abs(xdt)))
    sm_ref[...] = jnp.where(lane < GATE_COLS, _sigmoid(sm), dt)


def in_proj(x2d, mod, w_in, dt_bias, seq):
    t, d = x2d.shape
    wq = w_in[:, :512].astype(BF16)
    wkv = w_in[:, 512:512 + KV_COLS].astype(BF16)
    o = 512 + KV_COLS
    wg = w_in[:, o:o + GATE_COLS]
    wz = w_in[:, o + GATE_COLS:o + GATE_COLS + 512].astype(BF16)
    o2 = o + GATE_COLS + 512
    wx = w_in[:, o2:o2 + 1024].astype(BF16)
    wdt = w_in[:, o2 + 1024:]
    wsm = jnp.zeros((d, LANE), F32).at[:, :GATE_COLS].set(wg).at[:, DT_LANE0:DT_LANE0 + N_SSM_HEADS].set(wdt)
    wsh, wsl = _split2(wsm)
    dtb = jnp.zeros((1, LANE), F32).at[0, DT_LANE0:DT_LANE0 + N_SSM_HEADS].set(dt_bias)
    tpb = seq // TM
    full = lambda shp: pl.BlockSpec(shp, lambda i: (0, 0))
    row = lambda n: pl.BlockSpec((TM, n), lambda i: (i, 0))
    return pl.pallas_call(
        _inproj_kernel,
        grid=(t // TM,),
        in_specs=[
            row(d),
            pl.BlockSpec((1, 6, d), lambda i: (i // tpb, 0, 0)),
            full((d, 512)), full((d, KV_COLS)), full((d, 512)), full((d, 1024)),
            full((d, LANE)), full((d, LANE)), full((1, LANE)),
        ],
        out_specs=[row(512), row(KV_COLS), row(512), row(1024), row(LANE)],
        out_shape=[
            jax.ShapeDtypeStruct((t, 512), BF16),
            jax.ShapeDtypeStruct((t, KV_COLS), BF16),
            jax.ShapeDtypeStruct((t, 512), F32),
            jax.ShapeDtypeStruct((t, 1024), F32),
            jax.ShapeDtypeStruct((t, LANE), F32),
        ],
        compiler_params=_cparams(("parallel",)),
        name="in_proj",
    )(x2d, mod, wq, wkv, wz, wx, wsh, wsl, dtb)


def _gelu_tanh(x):
    return 0.5 * x * (1.0 + jnp.tanh(math.sqrt(2.0 / math.pi) * (x + 0.044715 * (x * x * x))))


def _cmp_kernel(x_ref, w1e_ref, w1_ref, pos_ref, w2e_ref, o_ref):
    x = x_ref[0, 0]
    nc = x.shape[0]
    ab = _dot(x, w1e_ref[0])
    half = N_KV_HEADS * CMP_HIDDEN
    first = ab[:, :half]
    second = pltpu.roll(ab[:, half:], nc - 1, 0)
    ph, plo = _split2(pos_ref[0])
    pb = (_dot(ph, w1_ref[0]) + _dot(plo, w1_ref[0]))[0:1, :]
    pb = jnp.concatenate([pb] * N_KV_HEADS, axis=1)
    h = _gelu_tanh(first + second + pb)
    o_ref[0, 0] = _dot(h.astype(BF16), w2e_ref[0]).astype(BF16)


def compress(kv_raw, pos, w1, w2):
    _, b, s, _ = kv_raw.shape
    nc = s // CMP_STRIDE
    x = kv_raw.reshape(2, b, nc, CMP_STRIDE * 128)
    g = N_KV_HEADS
    w1r = w1.reshape(2, 2, CMP_STRIDE, HEAD_DIM, CMP_HIDDEN)
    eye = jnp.eye(g, dtype=F32)
    w1e = jnp.einsum('kjtdh,ga->ktgdjah', w1r, eye).reshape(2, CMP_STRIDE * g * HEAD_DIM, 2 * g * CMP_HIDDEN)
    w2e = jnp.einsum('khd,ga->kghad', w2, eye).reshape(2, g * CMP_HIDDEN, g * HEAD_DIM)
    posf = jnp.zeros((2, 8, CMP_BLOCK * HEAD_DIM), F32).at[:, 0, :].set(pos.reshape(2, -1))
    out = pl.pallas_call(
        _cmp_kernel,
        grid=(2, b),
        in_specs=[
            pl.BlockSpec((1, 1, nc, CMP_STRIDE * 128), lambda k, i: (k, i, 0, 0)),
            pl.BlockSpec((1,) + w1e.shape[1:], lambda k, i: (k, 0, 0)),
            pl.BlockSpec((1,) + w1.shape[1:], lambda k, i: (k, 0, 0)),
            pl.BlockSpec((1,) + posf.shape[1:], lambda k, i: (k, 0, 0)),
            pl.BlockSpec((1,) + w2e.shape[1:], lambda k, i: (k, 0, 0)),
        ],
        out_specs=pl.BlockSpec((1, 1, nc, 128), lambda k, i: (k, i, 0, 0)),
        out_shape=jax.ShapeDtypeStruct((2, b, nc, 128), BF16),
        compiler_params=_cparams(("parallel", "parallel")),
        name="compress",
    )(x, w1e.astype(BF16), w1.astype(BF16), posf, w2e.astype(BF16))
    return out


R_ROWS = ATT_GQA * TQ
WIN_KEYS = WINDOW + TQ
KAUG = LANE + HEAD_DIM


def _attn_kernel(q_ref, gate_ref, kc_ref, vc_ref, ovt_ref, kaug_ref, vaug_ref, kw_ref, vw_ref,
                 o_ref, qaug_ref, m_ref, acc_ref):
    i = pl.program_id(2)
    t0 = i * TQ
    q2 = q_ref[0]
    qs = jnp.concatenate([q2[:, r * HEAD_DIM:(r + 1) * HEAD_DIM] for r in range(ATT_GQA)], axis=0)
    tok = lax.broadcasted_iota(I32, (R_ROWS, 1), 0) % TQ
    t = t0 + tok

    kc = kc_ref[0, 0]
    nc = kc.shape[0]
    s = _dot_nt(qs, kc)
    cidx = lax.broadcasted_iota(I32, (1, nc), 1)
    p = _masked_softmax(s, (cidx * CMP_STRIDE + (CMP_BLOCK - 1)) <= t)
    o_cmp = _dot(p.astype(BF16), vc_ref[0, 0])

    psum = p[0:TQ] + p[TQ:2 * TQ] + p[2 * TQ:3 * TQ] + p[3 * TQ:4 * TQ]
    ph, plo = _split2(psum)
    ovt = ovt_ref[...]
    imp = _dot_nt(ovt, ph) + _dot_nt(ovt, plo)
    j = lax.broadcasted_iota(I32, (LANE, TQ), 0)
    jt = (t0 + lax.broadcasted_iota(I32, (LANE, TQ), 1)) // SEL_BLOCK
    forced = (j == 0) | (j == jt) | (j == jt - 1)
    valid = j <= jt
    score = jnp.where(valid, imp + jnp.where(forced, SEL_FORCE_BONUS, 0.0), NEG)

    def pick(_, carry):
        sc, sel = carry
        mx = jnp.max(sc, axis=0, keepdims=True)
        idx = jnp.min(jnp.where(sc == mx, j, LANE), axis=0, keepdims=True)
        hit = j == idx
        return jnp.where(hit, -3.0e38, sc), jnp.where(hit, 1.0, sel)

    _, sel = lax.fori_loop(0, SEL_TOPN, pick, (score, jnp.zeros((LANE, TQ), F32)))
    bias_t = jnp.where(valid, jnp.where(sel > 0.0, 0.0, NEG), NEG)
    bias = bias_t.T.astype(BF16)
    for r in range(ATT_GQA):
        qaug_ref[r * TQ:(r + 1) * TQ, 0:LANE] = bias
        qaug_ref[r * TQ:(r + 1) * TQ, LANE:KAUG] = q2[:, r * HEAD_DIM:(r + 1) * HEAD_DIM]

    m_ref[...] = jnp.full((R_ROWS, 1), NEG, F32)
    acc_ref[...] = jnp.zeros((R_ROWS, LANE), F32)
    n_kv = (t0 + TQ + TK - 1) // TK

    def kv_step(kt, _):
        k0 = pl.multiple_of(kt * TK, TK)
        sc = _dot_nt(qaug_ref[...], kaug_ref[0, 0, pl.ds(k0, TK), :])
        kpos = k0 + lax.broadcasted_iota(I32, (1, TK), 1)
        sc = jnp.where(kpos <= t, sc, NEG)
        m_old = m_ref[...]
        m_new = jnp.maximum(m_old, jnp.max(sc, axis=-1, keepdims=True))
        pe = jnp.exp(sc - m_new)
        acc_ref[...] = jnp.exp(m_old - m_new) * acc_ref[...] + _dot(
            pe.astype(BF16), vaug_ref[0, 0, pl.ds(k0, TK), :])
        m_ref[...] = m_new
        return 0

    lax.fori_loop(0, n_kv, kv_step, 0)
    acc = acc_ref[...]
    o_sel = acc[:, :HEAD_DIM] / acc[:, HEAD_DIM:HEAD_DIM + 1]

    ws = pl.multiple_of(jnp.maximum(t0 - WINDOW, 0), TQ)
    sw = _dot_nt(qs, kw_ref[0, 0, pl.ds(ws, WIN_KEYS), :])
    wpos = ws + lax.broadcasted_iota(I32, (1, WIN_KEYS), 1)
    pw = _masked_softmax(sw, (wpos <= t) & (wpos > t - WINDOW))
    o_win = _dot(pw.astype(BF16), vw_ref[0, 0, pl.ds(ws, WIN_KEYS), :])

    gq = gate_ref[0, 0]
    outs = []
    for r in range(ATT_GQA):
        sl = slice(r * TQ, (r + 1) * TQ)
        outs.append(gq[:, 3 * r:3 * r + 1] * o_cmp[sl] + gq[:, 3 * r + 1:3 * r + 2] * o_sel[sl]
                    + gq[:, 3 * r + 2:3 * r + 3] * o_win[sl])
    o_ref[0] = jnp.concatenate(outs, axis=1)


def nsa_attention(q, gates, kc, vc, ks, vs, kw, vw):
    b, s, _ = q.shape
    g = N_KV_HEADS
    nc = kc.shape[2]
    n_sel = s // SEL_BLOCK
    assert n_sel <= LANE and s % TK == 0 and s >= WIN_KEYS
    cs = np.arange(nc)[None, :] * CMP_STRIDE
    js = np.arange(LANE)[:, None] * SEL_BLOCK
    ovt = ((cs < js + SEL_BLOCK) & (cs + CMP_BLOCK > js) & (np.arange(nc)[None, :] < nc - 1)
           & (np.arange(LANE)[:, None] < n_sel))
    ovt = jnp.asarray(ovt, BF16)
    onehot = jnp.asarray(np.arange(s)[:, None] // SEL_BLOCK == np.arange(LANE)[None, :], BF16)
    kaug = jnp.concatenate([jnp.broadcast_to(onehot, (b, g, s, LANE)), ks], axis=-1)
    ones = jnp.zeros((s, LANE - HEAD_DIM), BF16).at[:, 0].set(1)
    vaug = jnp.concatenate([vs, jnp.broadcast_to(ones, (b, g, s, LANE - HEAD_DIM))], axis=-1)
    per_bg = lambda n, w: pl.BlockSpec((1, 1, n, w), lambda bi, gi, i: (bi, gi, 0, 0))
    return pl.pallas_call(
        _attn_kernel,
        grid=(b, g, s // TQ),
        in_specs=[
            pl.BlockSpec((1, TQ, ATT_GQA * HEAD_DIM), lambda bi, gi, i: (bi, i, gi)),
            pl.BlockSpec((1, 1, TQ, 3 * ATT_GQA), lambda bi, gi, i: (bi, gi, i, 0)),
            per_bg(nc, HEAD_DIM), per_bg(nc, HEAD_DIM),
            pl.BlockSpec((LANE, nc), lambda bi, gi, i: (0, 0)),
            per_bg(s, KAUG), per_bg(s, LANE), per_bg(s, HEAD_DIM), per_bg(s, HEAD_DIM),
        ],
        out_specs=pl.BlockSpec((1, TQ, ATT_GQA * HEAD_DIM), lambda bi, gi, i: (bi, i, gi)),
        out_shape=jax.ShapeDtypeStruct((b, s, ATT_WIDTH), F32),
        scratch_shapes=[
            pltpu.VMEM((R_ROWS, KAUG), BF16),
            pltpu.VMEM((R_ROWS, 1), F32),
            pltpu.VMEM((R_ROWS, LANE), F32),
        ],
        compiler_params=_cparams(("parallel", "parallel", "arbitrary")),
        name="nsa_attention",
    )(q, gates, kc, vc, ovt, kaug, vaug, kw, vw)


def _ssd_kernel(xc_ref, xp_ref, z_ref, sm_ref, cw_ref, cb_ref, alog_ref, dsk_ref, nw_ref, o_ref, st_ref):
    c = pl.program_id(1)

    @pl.when(c == 0)
    def _():
        st_ref[...] = jnp.zeros_like(st_ref)

    q = SSM_CHUNK
    xcur = xc_ref[0]
    xprev = xp_ref[0] * (c > 0).astype(F32)
    row = lax.broadcasted_iota(I32, (q, 1), 0)
    cw = cw_ref[...]
    acc = cb_ref[...] + cw[SSM_CONV - 1:SSM_CONV, :] * xcur
    for s in range(1, SSM_CONV):
        shifted = jnp.where(row >= s, pltpu.roll(xcur, s, 0), pltpu.roll(xprev, s, 0))
        acc = acc + cw[SSM_CONV - 1 - s:SSM_CONV - s, :] * shifted
    xc = _silu(acc)
    xs = xc[:, :SSM_WIDTH]
    bm = xc[:, SSM_WIDTH:SSM_WIDTH + 2 * SSM_STATE]
    cm = xc[:, SSM_WIDTH + 2 * SSM_STATE:]

    sm = sm_ref[0]
    lane = lax.broadcasted_iota(I32, (1, LANE), 1)
    head_lane = (lane >= DT_LANE0) & (lane < DT_LANE0 + N_SSM_HEADS)
    da = jnp.where(head_lane, sm * (-jnp.exp(alog_ref[...])), 0.0)
    ri = lax.broadcasted_iota(I32, (q, q), 0)
    ci = lax.broadcasted_iota(I32, (q, q), 1)
    causal = ri >= ci
    tri = jnp.where(causal, 1.0, 0.0).astype(BF16)
    d1, d2, d3 = _split3(da)
    acum = _dot(tri, d1) + _dot(tri, d2) + _dot(tri, d3)
    acum_t = acum.T
    last = acum[q - 1:q, :]
    eac = jnp.exp(acum)
    to_end = jnp.exp(last - acum)
    elast = jnp.exp(last)

    hpg = N_SSM_HEADS // 2
    ys = []
    for g in range(2):
        bg = bm[:, g * SSM_STATE:(g + 1) * SSM_STATE]
        cg = cm[:, g * SSM_STATE:(g + 1) * SSM_STATE]
        gmat = _dot_nt(cg.astype(BF16), bg.astype(BF16))
        bgt = bg.T.astype(BF16)
        for hh in range(hpg):
            h = g * hpg + hh
            ln = DT_LANE0 + h
            x_h = xs[:, h * HEAD_DIM:(h + 1) * HEAD_DIM]
            xdt = x_h * sm[:, ln:ln + 1]
            dec = jnp.exp(jnp.where(causal, acum[:, ln:ln + 1] - acum_t[ln:ln + 1, :], NEG))
            st = st_ref[h]
            y = _dot((gmat * dec).astype(BF16), xdt.astype(BF16)) + _dot(
                (cg * eac[:, ln:ln + 1]).astype(BF16), st.astype(BF16))
            st_ref[h] = elast[:, ln:ln + 1] * st + _dot(bgt, (xdt * to_end[:, ln:ln + 1]).astype(BF16))
            ys.append(y + dsk_ref[:, h * HEAD_DIM:(h + 1) * HEAD_DIM] * x_h)
    y = jnp.concatenate(ys, axis=1)
    yz = y * _silu(z_ref[0])
    half = SSM_WIDTH // 2
    outs = []
    for grp in range(2):
        v = yz[:, grp * half:(grp + 1) * half]
        outs.append(v * lax.rsqrt(jnp.mean(v * v, axis=-1, keepdims=True) + RMS_EPS)
                    * nw_ref[:, grp * half:(grp + 1) * half])
    o_ref[0] = jnp.concatenate(outs, axis=1).astype(BF16)


def ssd_mixer(xbc, z, sm, conv_w, conv_b, a_log, d_skip, ssm_norm_w, b, s):
    q = SSM_CHUNK
    cdim = xbc.shape[-1]
    alog = jnp.zeros((1, LANE), F32).at[0, DT_LANE0:DT_LANE0 + N_SSM_HEADS].set(a_log)
    dsk = jnp.repeat(d_skip, HEAD_DIM)[None, :]
    blk = lambda w: pl.BlockSpec((1, q, w), lambda bi, c: (bi, c, 0))
    full = lambda shp: pl.BlockSpec(shp, lambda bi, c: (0, 0))
    return pl.pallas_call(
        _ssd_kernel,
        grid=(b, s // q),
        in_specs=[
            blk(cdim),
            pl.BlockSpec((1, q, cdim), lambda bi, c: (bi, jnp.maximum(c - 1, 0), 0)),
            blk(SSM_WIDTH), blk(LANE),
            full((SSM_CONV, cdim)), full((1, cdim)), full((1, LANE)), full((1, SSM_WIDTH)), full((1, SSM_WIDTH)),
        ],
        out_specs=blk(SSM_WIDTH),
        out_shape=jax.ShapeDtypeStruct((b, s, SSM_WIDTH), BF16),
        scratch_shapes=[pltpu.VMEM((N_SSM_HEADS, SSM_STATE, HEAD_DIM), F32)],
        compiler_params=_cparams(("parallel", "arbitrary")),
        name="ssd_mixer",
    )(xbc.reshape(b, s, cdim), xbc.reshape(b, s, cdim), z.reshape(b, s, SSM_WIDTH), sm.reshape(b, s, LANE),
      conv_w, conv_b[None, :], alog, dsk, ssm_norm_w[None, :])


def _outproj_kernel(att_ref, ssm_ref, x_ref, mod_ref, anw_ref, wa_ref, ws_ref, lnw_ref, lnb_ref,
                    x1_ref, u2_ref):
    att = att_ref[...]
    an = att * lax.rsqrt(jnp.mean(att * att, axis=-1, keepdims=True) + RMS_EPS) * anw_ref[...]
    h = _dot(an.astype(BF16), wa_ref[...]) + _dot(ssm_ref[...], ws_ref[...])
    gate1 = mod_ref[0, 2:3, :]
    x1 = _ln_rows(DN_ALPHA * x_ref[...] + (1.0 + gate1) * h) * lnw_ref[...] + lnb_ref[...]
    x1_ref[...] = x1
    shift2 = mod_ref[0, 3:4, :]
    scale2 = mod_ref[0, 4:5, :]
    u2_ref[...] = (_ln_rows(x1) * (1.0 + scale2) + shift2).astype(BF16)


def out_proj(att, ssm, x2d, mod, attn_norm_w, w_out, ln_w, ln_b, seq):
    t, d = x2d.shape
    tpb = seq // TM
    wo = w_out.astype(BF16)
    full = lambda shp: pl.BlockSpec(shp, lambda i: (0, 0))
    row = lambda n: pl.BlockSpec((TM, n), lambda i: (i, 0))
    return pl.pallas_call(
        _outproj_kernel,
        grid=(t // TM,),
        in_specs=[
            row(ATT_WIDTH), row(SSM_WIDTH), row(d),
            pl.BlockSpec((1, 6, d), lambda i: (i // tpb, 0, 0)),
            full((1, ATT_WIDTH)), full((ATT_WIDTH, d)), full((SSM_WIDTH, d)), full((1, d)), full((1, d)),
        ],
        out_specs=[row(d), row(d)],
        out_shape=[jax.ShapeDtypeStruct((t, d), F32), jax.ShapeDtypeStruct((t, d), BF16)],
        compiler_params=_cparams(("parallel",)),
        name="out_proj",
    )(att, ssm, x2d, mod, attn_norm_w[None, :], wo[:ATT_WIDTH], wo[ATT_WIDTH:], ln_w[None, :], ln_b[None, :])


def _chunks(n, c):
    return [(f0, min(c, n - f0)) for f0 in range(0, n, c)]


def _ffn_kernel(u_ref, x_ref, mod_ref, wg_ref, wu_ref, wd_ref, lnw_ref, lnb_ref, o_ref, acc_ref):
    u = u_ref[...]
    for k, (f0, fc) in enumerate(_chunks(wg_ref.shape[1], FCHUNK)):
        hid = (_silu(_dot(u, wg_ref[:, f0:f0 + fc])) * _dot(u, wu_ref[:, f0:f0 + fc])).astype(BF16)
        part = _dot(hid, wd_ref[f0:f0 + fc, :])
        if k == 0:
            acc_ref[...] = part
        else:
            acc_ref[...] += part
    gate2 = mod_ref[0, 5:6, :]
    o_ref[...] = _ln_rows(DN_ALPHA * x_ref[...] + (1.0 + gate2) * acc_ref[...]) * lnw_ref[...] + lnb_ref[...]


def dense_ffn(u2, x1, mod, wg, wu, wd, ln_w, ln_b, seq):
    t, d = x1.shape
    f = wg.shape[1]
    tpb = seq // TM
    full = lambda shp: pl.BlockSpec(shp, lambda i: (0, 0))
    row = lambda n: pl.BlockSpec((TM, n), lambda i: (i, 0))
    return pl.pallas_call(
        _ffn_kernel,
        grid=(t // TM,),
        in_specs=[
            row(d), row(d),
            pl.BlockSpec((1, 6, d), lambda i: (i // tpb, 0, 0)),
            full((d, f)), full((d, f)), full((f, d)), full((1, d)), full((1, d)),
        ],
        out_specs=row(d),
        out_shape=jax.ShapeDtypeStruct((t, d), F32),
        scratch_shapes=[pltpu.VMEM((TM, d), F32)],
        compiler_params=_cparams(("parallel",)),
        name="dense_ffn",
    )(u2, x1, mod, wg.astype(BF16), wu.astype(BF16), wd.astype(BF16), ln_w[None, :], ln_b[None, :])


MOE_WIN = MOE_TILE


def _router_kernel(u_ref, rw_ref, g_ref, mt_ref, cnt_ref):
    lt = _dot_nt(rw_ref[...], u_ref[...])
    e = lax.broadcasted_iota(I32, lt.shape, 0)
    top1 = jnp.max(lt, axis=0, keepdims=True)
    m1 = e == jnp.min(jnp.where(lt == top1, e, N_EXPERTS), axis=0, keepdims=True)
    l2 = jnp.where(m1, -3.0e38, lt)
    top2 = jnp.max(l2, axis=0, keepdims=True)
    m2 = e == jnp.min(jnp.where(l2 == top2, e, N_EXPERTS), axis=0, keepdims=True)
    e2 = jnp.exp(top2 - top1)
    den = 1.0 + e2
    gates = jnp.where(m1, 1.0 / den, 0.0) + jnp.where(m2, e2 / den, 0.0)
    member = jnp.where(m1 | m2, 1.0, 0.0)
    pad = jnp.zeros((LANE - 2 * N_EXPERTS, lt.shape[1]), F32)
    g_ref[...] = jnp.concatenate([gates, member, pad], axis=0).T
    mt_ref[...] = member
    cnt_ref[0] = jnp.broadcast_to(jnp.sum(member, axis=1, keepdims=True), (N_EXPERTS, LANE))


def moe_router(u2, router_w):
    t, d = u2.shape
    nt = t // MOE_TILE
    return pl.pallas_call(
        _router_kernel,
        grid=(nt,),
        in_specs=[pl.BlockSpec((MOE_TILE, d), lambda i: (i, 0)), pl.BlockSpec((N_EXPERTS, d), lambda i: (0, 0))],
        out_specs=[
            pl.BlockSpec((MOE_TILE, LANE), lambda i: (i, 0)),
            pl.BlockSpec((N_EXPERTS, MOE_TILE), lambda i: (0, i)),
            pl.BlockSpec((1, N_EXPERTS, LANE), lambda i: (i, 0, 0)),
        ],
        out_shape=[
            jax.ShapeDtypeStruct((t, LANE), F32),
            jax.ShapeDtypeStruct((N_EXPERTS, t), F32),
            jax.ShapeDtypeStruct((nt, N_EXPERTS, LANE), F32),
        ],
        compiler_params=_cparams(("parallel",)),
        name="moe_router",
    )(u2, router_w.T.astype(BF16))


def _dispatch_copies(base_ref, buf_ref, xs_ref, sem_ref, step, slot):
    return [
        pltpu.make_async_copy(
            buf_ref.at[slot, e],
            xs_ref.at[pl.ds(pl.multiple_of(base_ref[step * N_EXPERTS + e], MOE_ALIGN), MOE_WIN)],
            sem_ref.at[slot])
        for e in range(N_EXPERTS)
    ]


def _dispatch_kernel(base_ref, u_ref, mt_ref, xs_in_ref, xs_ref, buf_ref, sem_ref):
    del xs_in_ref
    i = pl.program_id(0)
    slot = i % 2
    mt = mt_ref[...]
    n = mt.shape[1]
    ri = lax.broadcasted_iota(I32, (n, n), 0)
    ci = lax.broadcasted_iota(I32, (n, n), 1)
    before = jnp.where(ri < ci, 1.0, 0.0).astype(BF16)
    rank = _dot(mt.astype(BF16), before).astype(I32)
    u = u_ref[...]
    wr = lax.broadcasted_iota(I32, (MOE_WIN, n), 0)
    for e in range(N_EXPERTS):
        pick = jnp.where((wr == rank[e:e + 1, :]) & (mt[e:e + 1, :] > 0.0), 1.0, 0.0).astype(BF16)
        buf_ref[slot, e] = _dot(pick, u).astype(BF16)

    @pl.when(i > 0)
    def _():
        for cp in _dispatch_copies(base_ref, buf_ref, xs_ref, sem_ref, i - 1, 1 - slot):
            cp.wait()

    for cp in _dispatch_copies(base_ref, buf_ref, xs_ref, sem_ref, i, slot):
        cp.start()

    @pl.when(i == pl.num_programs(0) - 1)
    def _():
        for cp in _dispatch_copies(base_ref, buf_ref, xs_ref, sem_ref, i, slot):
            cp.wait()


def moe_dispatch(u2, member_t, base, n_rows):
    t, d = u2.shape
    nt = t // MOE_TILE
    xs0 = jnp.zeros((n_rows, d), BF16)
    grid_spec = pltpu.PrefetchScalarGridSpec(
        num_scalar_prefetch=1,
        grid=(nt,),
        in_specs=[
            pl.BlockSpec((MOE_TILE, d), lambda i, base: (i, 0)),
            pl.BlockSpec((N_EXPERTS, MOE_TILE), lambda i, base: (0, i)),
            pl.BlockSpec(memory_space=pl.ANY),
        ],
        out_specs=pl.BlockSpec(memory_space=pl.ANY),
        scratch_shapes=[
            pltpu.VMEM((2, N_EXPERTS, MOE_WIN, d), BF16),
            pltpu.SemaphoreType.DMA((2,)),
        ],
    )
    return pl.pallas_call(
        _dispatch_kernel,
        grid_spec=grid_spec,
        out_shape=jax.ShapeDtypeStruct((n_rows, d), BF16),
        input_output_aliases={3: 0},
        compiler_params=_cparams(("arbitrary",)),
        name="moe_dispatch",
    )(base.reshape(-1), u2, member_t, xs0)


def _expert_kernel(be_ref, act_ref, x_ref, wg_ref, wu_ref, wd_ref, y_ref, acc_ref):
    i = pl.program_id(0)

    @pl.when(act_ref[i] > 0)
    def _():
        x = x_ref[...]
        for k, (f0, fc) in enumerate(_chunks(wg_ref.shape[2], FCHUNK)):
            hid = (_silu(_dot(x, wg_ref[0, :, f0:f0 + fc])) * _dot(x, wu_ref[0, :, f0:f0 + fc])).astype(BF16)
            part = _dot(hid, wd_ref[0, f0:f0 + fc, :])
            if k == 0:
                acc_ref[...] = part
            else:
                acc_ref[...] += part
        y_ref[...] = acc_ref[...].astype(BF16)

    @pl.when(act_ref[i] == 0)
    def _():
        y_ref[...] = jnp.zeros_like(y_ref)


def moe_experts(xs, blk_expert, blk_active, wg, wu, wd):
    n_rows, d = xs.shape
    f = wg.shape[2]
    resident = dict(pipeline_mode=pl.Buffered(1))
    grid_spec = pltpu.PrefetchScalarGridSpec(
        num_scalar_prefetch=2,
        grid=(n_rows // MOE_BLK,),
        in_specs=[
            pl.BlockSpec((MOE_BLK, d), lambda i, be, act: (i, 0)),
            pl.BlockSpec((1, d, f), lambda i, be, act: (be[i], 0, 0), **resident),
            pl.BlockSpec((1, d, f), lambda i, be, act: (be[i], 0, 0), **resident),
            pl.BlockSpec((1, f, d), lambda i, be, act: (be[i], 0, 0), **resident),
        ],
        out_specs=pl.BlockSpec((MOE_BLK, d), lambda i, be, act: (i, 0)),
        scratch_shapes=[pltpu.VMEM((MOE_BLK, d), F32)],
    )
    return pl.pallas_call(
        _expert_kernel,
        grid_spec=grid_spec,
        out_shape=jax.ShapeDtypeStruct((n_rows, d), BF16),
        compiler_params=_cparams(("arbitrary",)),
        name="moe_experts",
    )(blk_expert, blk_active, xs, wg.astype(BF16), wu.astype(BF16), wd.astype(BF16))


def _combine_copies(base_ref, y_ref, buf_ref, sem_ref, step, slot):
    return [
        pltpu.make_async_copy(
            y_ref.at[pl.ds(pl.multiple_of(base_ref[step * N_EXPERTS + e], MOE_ALIGN), MOE_WIN)],
            buf_ref.at[slot, e],
            sem_ref.at[slot])
        for e in range(N_EXPERTS)
    ]


def _combine_kernel(base_ref, g_ref, x_ref, mod_ref, lnw_ref, lnb_ref, y_ref, o_ref, buf_ref, sem_ref):
    i = pl.program_id(0)
    slot = i % 2

    @pl.when(i == 0)
    def _():
        for cp in _combine_copies(base_ref, y_ref, buf_ref, sem_ref, 0, 0):
            cp.start()

    @pl.when(i + 1 < pl.num_programs(0))
    def _():
        for cp in _combine_copies(base_ref, y_ref, buf_ref, sem_ref, i + 1, 1 - slot):
            cp.start()

    gm = g_ref[...]
    n = gm.shape[0]
    ri = lax.broadcasted_iota(I32, (n, n), 0)
    ci = lax.broadcasted_iota(I32, (n, n), 1)
    before = jnp.where(ci < ri, 1.0, 0.0).astype(BF16)
    lane = lax.broadcasted_iota(I32, (1, LANE), 1)
    member = jnp.where((lane >= N_EXPERTS) & (lane < 2 * N_EXPERTS), gm, 0.0)
    rank = _dot(before, member.astype(BF16)).astype(I32)
    wc = lax.broadcasted_iota(I32, (n, MOE_WIN), 1)

    for cp in _combine_copies(base_ref, y_ref, buf_ref, sem_ref, i, slot):
        cp.wait()

    f = jnp.zeros((n, x_ref.shape[1]), F32)
    for e in range(N_EXPERTS):
        le = N_EXPERTS + e
        pick = jnp.where((wc == rank[:, le:le + 1]) & (gm[:, le:le + 1] > 0.0), 1.0, 0.0).astype(BF16)
        f = f + gm[:, e:e + 1] * _dot(pick, buf_ref[slot, e])
    gate2 = mod_ref[0, 5:6, :]
    o_ref[...] = _ln_rows(DN_ALPHA * x_ref[...] + (1.0 + gate2) * f) * lnw_ref[...] + lnb_ref[...]


def moe_combine(y, gates, base, x1, mod, ln_w, ln_b, seq):
    t, d = x1.shape
    nt = t // MOE_TILE
    tpb = seq // MOE_TILE
    grid_spec = pltpu.PrefetchScalarGridSpec(
        num_scalar_prefetch=1,
        grid=(nt,),
        in_specs=[
            pl.BlockSpec((MOE_TILE, LANE), lambda i, base: (i, 0)),
            pl.BlockSpec((MOE_TILE, d), lambda i, base: (i, 0)),
            pl.BlockSpec((1, 6, d), lambda i, base: (i // tpb, 0, 0)),
            pl.BlockSpec((1, d), lambda i, base: (0, 0)),
            pl.BlockSpec((1, d), lambda i, base: (0, 0)),
            pl.BlockSpec(memory_space=pl.ANY),
        ],
        out_specs=pl.BlockSpec((MOE_TILE, d), lambda i, base: (i, 0)),
        scratch_shapes=[
            pltpu.VMEM((2, N_EXPERTS, MOE_WIN, d), BF16),
            pltpu.SemaphoreType.DMA((2,)),
        ],
    )
    return pl.pallas_call(
        _combine_kernel,
        grid_spec=grid_spec,
        out_shape=jax.ShapeDtypeStruct((t, d), F32),
        compiler_params=_cparams(("arbitrary",)),
        name="moe_combine",
    )(base.reshape(-1), gates, x1, mod, ln_w[None, :], ln_b[None, :], y)


def moe_ffn(u2, x1, mod, router_w, wg, wu, wd, ln_w, ln_b, seq):
    t, d = u2.shape
    nt = t // MOE_TILE
    gates, member_t, cnt = moe_router(u2, router_w)
    cnt = cnt[:, :, 0].astype(I32)
    seg = (cnt + MOE_ALIGN - 1) // MOE_ALIGN * MOE_ALIGN
    region = (jnp.sum(seg, axis=0) + MOE_WIN + MOE_BLK - 1) // MOE_BLK * MOE_BLK
    region_end = jnp.cumsum(region)
    base = (region_end - region)[None, :] + jnp.cumsum(seg, axis=0) - seg
    n_blk = -(-(2 * t + nt * N_EXPERTS * (MOE_ALIGN - 1) + N_EXPERTS * (MOE_WIN + MOE_BLK - 1)) // MOE_BLK)
    blk_start = jnp.arange(n_blk, dtype=I32) * MOE_BLK
    blk_expert = jnp.minimum(jnp.searchsorted(region_end, blk_start, side='right'), N_EXPERTS - 1).astype(I32)
    blk_active = (blk_start < region_end[-1]).astype(I32)
    xs = moe_dispatch(u2, member_t, base.astype(I32), n_blk * MOE_BLK)
    y = moe_experts(xs, blk_expert, blk_active, wg, wu, wd)
    return moe_combine(y, gates, base.astype(I32), x1, mod, ln_w, ln_b, seq)


def _mixer_front(x, mod, w_in, dt_bias, cmp_pos_k, cmp_w1_k, cmp_w2_k, cmp_pos_v, cmp_w1_v, cmp_w2_v):
    b, s, d = x.shape
    t = b * s
    g = N_KV_HEADS
    q, kv, z, xbc, sm = in_proj(x.reshape(t, d), mod, w_in, dt_bias, s)
    kv = kv.reshape(b, s, 6, 128)
    kvc = compress(jnp.stack([kv[:, :, 0], kv[:, :, 1]]),
                   jnp.stack([cmp_pos_k, cmp_pos_v]), jnp.stack([cmp_w1_k, cmp_w1_v]),
                   jnp.stack([cmp_w2_k, cmp_w2_v]))
    heads = lambda a: a.reshape(b, -1, g, HEAD_DIM).transpose(0, 2, 1, 3)
    gates = sm[:, :GATE_COLS].reshape(b, s, g, 3 * ATT_GQA).transpose(0, 2, 1, 3)
    att = nsa_attention(q.reshape(b, s, ATT_WIDTH), gates, heads(kvc[0]), heads(kvc[1]),
                        heads(kv[:, :, 2]), heads(kv[:, :, 3]), heads(kv[:, :, 4]), heads(kv[:, :, 5]))
    return att, z, xbc, sm


def kernel(x, c, ada_w, ada_b, w_in, cmp_pos_k, cmp_w1_k, cmp_w2_k, cmp_pos_v, cmp_w1_v, cmp_w2_v, attn_norm_w, conv_w, conv_b, dt_bias, a_log, d_skip, ssm_norm_w, w_out, ln1_w, ln1_b, ln2_w, ln2_b, ffn_w_gate, ffn_w_up, ffn_w_down, router_w, exp_w_gate, exp_w_up, exp_w_down):
    b, s, d = x.shape
    t = b * s
    mod = ada_mod(c, ada_w, ada_b).reshape(DEPTH, b, 6, d)
    xc = x.reshape(t, d)
    for l in range(DEPTH):
        att, z, xbc, sm = _mixer_front(xc.reshape(b, s, d), mod[l], w_in[l], dt_bias[l],
                                       cmp_pos_k[l], cmp_w1_k[l], cmp_w2_k[l],
                                       cmp_pos_v[l], cmp_w1_v[l], cmp_w2_v[l])
        ssm = ssd_mixer(xbc, z, sm, conv_w[l], conv_b[l], a_log[l], d_skip[l], ssm_norm_w[l], b, s)
        x1, u2 = out_proj(att.reshape(t, ATT_WIDTH), ssm.reshape(t, SSM_WIDTH), xc, mod[l], attn_norm_w[l],
                          w_out[l], ln1_w[l], ln1_b[l], s)
        i = l // 2
        if l % 2 == 0:
            xc = dense_ffn(u2, x1, mod[l], ffn_w_gate[i], ffn_w_up[i], ffn_w_down[i], ln2_w[l], ln2_b[l], s)
        else:
            xc = moe_ffn(u2, x1, mod[l], router_w[i], exp_w_gate[i], exp_w_up[i], exp_w_down[i],
                         ln2_w[l], ln2_b[l], s)
    return xc.reshape(b, s, d)
```

```python
import functools
import math

import numpy as np
import jax
import jax.numpy as jnp
from jax import lax
from jax.experimental import pallas as pl
from jax.experimental.pallas import tpu as pltpu

F32 = jnp.float32
BF16 = jnp.bfloat16
I32 = jnp.int32

D_MODEL = 1024
HEAD_DIM = 64
ATT_WIDTH = 512
N_KV_HEADS = 2
ATT_GQA = 4
CMP_BLOCK = 32
CMP_STRIDE = 16
CMP_HIDDEN = 256
SEL_BLOCK = 64
SEL_TOPN = 16
SEL_FORCE_BONUS = 1e4
WINDOW = 512
SSM_WIDTH = 512
N_SSM_HEADS = 8
SSM_STATE = 128
SSM_CONV = 4
SSM_CHUNK = 128
FFN_DENSE = 2816
N_EXPERTS = 8
FFN_EXPERT = 3584
DEPTH = 2
DN_ALPHA = (2 * DEPTH) ** 0.25
LN_EPS = 1e-5
RMS_EPS = 1e-6
NEG = -1e30

LANE = 128
VMEM_LIMIT = 56 * 1024 * 1024

TM = 512
TQ = 128
TK = 512
MOE_TILE = 512
MOE_ALIGN = 16
MOE_BLK = 1024
FCHUNK = 512


def _cparams(sem):
    return pltpu.CompilerParams(dimension_semantics=sem, vmem_limit_bytes=VMEM_LIMIT)


def _dot(a, b):
    return jnp.dot(a, b, preferred_element_type=F32)


def _dot_nt(a, b):
    return lax.dot_general(a, b, (((1,), (1,)), ((), ())), preferred_element_type=F32)


def _split2(x):
    hi = x.astype(BF16)
    lo = (x - hi.astype(F32)).astype(BF16)
    return hi, lo


def _split3(x):
    hi = x.astype(BF16)
    r = x - hi.astype(F32)
    mid = r.astype(BF16)
    lo = (r - mid.astype(F32)).astype(BF16)
    return hi, mid, lo


def _sigmoid(x):
    return 1.0 / (1.0 + jnp.exp(-x))


def _silu(x):
    return x * _sigmoid(x)


def _ln_rows(x):
    mu = jnp.mean(x, axis=-1, keepdims=True)
    xc = x - mu
    var = jnp.mean(xc * xc, axis=-1, keepdims=True)
    return xc * lax.rsqrt(var + LN_EPS)


def _ada_kernel(c_ref, w_ref, b_ref, o_ref):
    cond = _silu(c_ref[...])
    ch, cl = _split2(cond)
    wh, wl = _split2(w_ref[0])
    o_ref[0] = _dot(ch, wh) + _dot(cl, wh) + _dot(ch, wl) + b_ref[0]


def ada_mod(c, ada_w, ada_b):
    nl, d, n = ada_w.shape
    b = c.shape[0]
    tn = 1536
    return pl.pallas_call(
        _ada_kernel,
        grid=(nl, n // tn),
        in_specs=[
            pl.BlockSpec((b, d), lambda l, j: (0, 0)),
            pl.BlockSpec((1, d, tn), lambda l, j: (l, 0, j)),
            pl.BlockSpec((1, 1, tn), lambda l, j: (l, 0, j)),
        ],
        out_specs=pl.BlockSpec((1, b, tn), lambda l, j: (l, 0, j)),
        out_shape=jax.ShapeDtypeStruct((nl, b, n), F32),
        compiler_params=_cparams(("parallel", "parallel")),
        name="ada_mod",
    )(c, ada_w, ada_b.reshape(nl, 1, n))


KV_COLS = 6 * 128
GATE_COLS = 24
DT_LANE0 = 32


def _inproj_kernel(x_ref, mod_ref, wq_ref, wkv_ref, wz_ref, wx_ref, wsh_ref, wsl_ref, dtb_ref,
                   q_ref, kv_ref, z_ref, xbc_ref, sm_ref):
    xn = _ln_rows(x_ref[...])
    shift = mod_ref[0, 0:1, :]
    scale = mod_ref[0, 1:2, :]
    u = xn * (1.0 + scale) + shift
    ub = u.astype(BF16)
    ul = (u - ub.astype(F32)).astype(BF16)
    q_ref[...] = (_dot(ub, wq_ref[...]) * (LOG2E * HEAD_DIM ** -0.5)).astype(BF16)
    kv_ref[...] = _dot(ub, wkv_ref[...]).astype(BF16)
    z_ref[...] = _dot(ub, wz_ref[...])
    xbc_ref[...] = _dot(ub, wx_ref[...])
    sm = _dot(ub, wsh_ref[...]) + _dot(ul, wsh_ref[...]) + _dot(ub, wsl_ref[...])
    lane = lax.broadcasted_iota(I32, sm.shape, 1)
    xdt = sm + dtb_ref[...]
    dt = jnp.maximum(xdt, 0.0) + jnp.log(1.0 + jnp.exp(-jnp.abs(xdt)))
    sm_ref[...] = jnp.where(lane < GATE_COLS, _sigmoid(sm), dt)


def in_proj(x2d, mod, w_in, dt_bias, seq):
    t, d = x2d.shape
    wq = w_in[:, :512].astype(BF16)
    wkv = w_in[:, 512:512 + KV_COLS].astype(BF16)
    o = 512 + KV_COLS
    wg = w_in[:, o:o + GATE_COLS]
    wz = w_in[:, o + GATE_COLS:o + GATE_COLS + 512].astype(BF16)
    o2 = o + GATE_COLS + 512
    wx = w_in[:, o2:o2 + 1024].astype(BF16)
    wdt = w_in[:, o2 + 1024:]
    wsm = jnp.zeros((d, LANE), F32).at[:, :GATE_COLS].set(wg).at[:, DT_LANE0:DT_LANE0 + N_SSM_HEADS].set(wdt)
    wsh, wsl = _split2(wsm)
    dtb = jnp.zeros((1, LANE), F32).at[0, DT_LANE0:DT_LANE0 + N_SSM_HEADS].set(dt_bias)
    tpb = seq // TM
    full = lambda shp: pl.BlockSpec(shp, lambda i: (0, 0))
    row = lambda n: pl.BlockSpec((TM, n), lambda i: (i, 0))
    return pl.pallas_call(
        _inproj_kernel,
        grid=(t // TM,),
        in_specs=[
            row(d),
            pl.BlockSpec((1, 6, d), lambda i: (i // tpb, 0, 0)),
            full((d, 512)), full((d, KV_COLS)), full((d, 512)), full((d, 1024)),
            full((d, LANE)), full((d, LANE)), full((1, LANE)),
        ],
        out_specs=[row(512), row(KV_COLS), row(512), row(1024), row(LANE)],
        out_shape=[
            jax.ShapeDtypeStruct((t, 512), BF16),
            jax.ShapeDtypeStruct((t, KV_COLS), BF16),
            jax.ShapeDtypeStruct((t, 512), F32),
            jax.ShapeDtypeStruct((t, 1024), F32),
            jax.ShapeDtypeStruct((t, LANE), F32),
        ],
        compiler_params=_cparams(("parallel",)),
        name="in_proj",
    )(x2d, mod, wq, wkv, wz, wx, wsh, wsl, dtb)


def _gelu_tanh(x):
    return 0.5 * x * (1.0 + jnp.tanh(math.sqrt(2.0 / math.pi) * (x + 0.044715 * (x * x * x))))


def _cmp_kernel(x_ref, w1e_ref, w1_ref, pos_ref, w2e_ref, o_ref):
    x = x_ref[0, 0]
    nc = x.shape[0]
    ab = _dot(x, w1e_ref[0])
    half = N_KV_HEADS * CMP_HIDDEN
    first = ab[:, :half]
    second = pltpu.roll(ab[:, half:], nc - 1, 0)
    ph, plo = _split2(pos_ref[0])
    pb = (_dot(ph, w1_ref[0]) + _dot(plo, w1_ref[0]))[0:1, :]
    pb = jnp.concatenate([pb] * N_KV_HEADS, axis=1)
    h = _gelu_tanh(first + second + pb)
    o_ref[0, 0] = _dot(h.astype(BF16), w2e_ref[0]).astype(BF16)


def compress(kv_raw, pos, w1, w2):
    _, b, s, _ = kv_raw.shape
    nc = s // CMP_STRIDE
    x = kv_raw.reshape(2, b, nc, CMP_STRIDE * 128)
    g = N_KV_HEADS
    w1r = w1.reshape(2, 2, CMP_STRIDE, HEAD_DIM, CMP_HIDDEN)
    eye = jnp.eye(g, dtype=F32)
    w1e = jnp.einsum('kjtdh,ga->ktgdjah', w1r, eye).reshape(2, CMP_STRIDE * g * HEAD_DIM, 2 * g * CMP_HIDDEN)
    w2e = jnp.einsum('khd,ga->kghad', w2, eye).reshape(2, g * CMP_HIDDEN, g * HEAD_DIM)
    posf = jnp.zeros((2, 8, CMP_BLOCK * HEAD_DIM), F32).at[:, 0, :].set(pos.reshape(2, -1))
    out = pl.pallas_call(
        _cmp_kernel,
        grid=(2, b),
        in_specs=[
            pl.BlockSpec((1, 1, nc, CMP_STRIDE * 128), lambda k, i: (k, i, 0, 0)),
            pl.BlockSpec((1,) + w1e.shape[1:], lambda k, i: (k, 0, 0)),
            pl.BlockSpec((1,) + w1.shape[1:], lambda k, i: (k, 0, 0)),
            pl.BlockSpec((1,) + posf.shape[1:], lambda k, i: (k, 0, 0)),
            pl.BlockSpec((1,) + w2e.shape[1:], lambda k, i: (k, 0, 0)),
        ],
        out_specs=pl.BlockSpec((1, 1, nc, 128), lambda k, i: (k, i, 0, 0)),
        out_shape=jax.ShapeDtypeStruct((2, b, nc, 128), BF16),
        compiler_params=_cparams(("parallel", "parallel")),
        name="compress",
    )(x, w1e.astype(BF16), w1.astype(BF16), posf, w2e.astype(BF16))
    return out


R_ROWS = ATT_GQA * TQ
WIN_KEYS = WINDOW + TQ
KAUG = LANE + HEAD_DIM
VROWS = HEAD_DIM + 16
LOG2E = math.log2(math.e)
SEL_PAD = TK - TQ


def _exp_cols(s):
    return jnp.exp2(s - jnp.max(s, axis=0, keepdims=True))


def _sel_scores(k0, width, qaug_ref, kaug_ref):
    return _dot(kaug_ref[0, 0, pl.ds(k0, width), :], qaug_ref[...])


def _sel_update(st, k0, m_old, vt_ref, acc_ref):
    m_new = jnp.maximum(m_old, jnp.max(st, axis=0, keepdims=True))
    pt = jnp.exp2(st - m_new).astype(BF16)
    c0 = k0 // LANE
    vt = jnp.concatenate([vt_ref[0, 0, c0 + c] for c in range(st.shape[0] // LANE)], axis=1)
    acc_ref[...] = jnp.exp2(m_old - m_new) * acc_ref[...] + _dot(vt, pt)
    return m_new


def _attn_kernel(q_ref, gate_ref, kc_ref, vct_ref, ovt_ref, cband_ref, wband_ref, kaug_ref, vt_ref, kw_ref, vwt_ref,
                 o_ref, qaug_ref, acc_ref, s0_ref, s1_ref, part_ref):
    i = pl.program_id(2)
    t0 = i * TQ
    qt = q_ref[0].astype(F32).T.astype(BF16)
    qst = jnp.concatenate([qt[r * HEAD_DIM:(r + 1) * HEAD_DIM, :] for r in range(ATT_GQA)], axis=1)
    t_lane = t0 + lax.broadcasted_iota(I32, (1, R_ROWS), 1) % TQ

    kc = kc_ref[0, 0]
    nc = kc.shape[0]
    cband = cband_ref[pl.ds(pl.multiple_of(nc - t0 // CMP_STRIDE, 8), nc), :]
    e = _exp_cols(_dot(kc, qst) + jnp.concatenate([cband] * ATT_GQA, axis=1))
    p = e * jnp.where(t_lane >= CMP_BLOCK - 1, 1.0 / jnp.sum(e, axis=0, keepdims=True), 0.0)
    o_cmp = _dot(vct_ref[0, 0], p.astype(BF16))

    ws = pl.multiple_of(jnp.maximum(t0 - WINDOW, 0), TQ)
    wband = wband_ref[pl.ds(pl.multiple_of(WINDOW - (t0 - ws), TQ), WIN_KEYS), :]
    ew = _exp_cols(_dot(kw_ref[0, 0, pl.ds(ws, WIN_KEYS), :], qst)
                   + jnp.concatenate([wband] * ATT_GQA, axis=1)).astype(BF16)
    cw = ws // LANE
    o_win = _dot(vwt_ref[0, 0, cw], ew[0:LANE])
    for c in range(1, WIN_KEYS // LANE):
        o_win = o_win + _dot(vwt_ref[0, 0, cw + c], ew[c * LANE:(c + 1) * LANE])
    o_win = o_win[:HEAD_DIM] * (1.0 / o_win[HEAD_DIM:HEAD_DIM + 1])

    gt = gate_ref[0, 0]
    for r in range(ATT_GQA):
        sl = slice(r * TQ, (r + 1) * TQ)
        part_ref[r * HEAD_DIM:(r + 1) * HEAD_DIM, :] = (gt[3 * r:3 * r + 1] * o_cmp[:, sl]
                                                        + gt[3 * r + 2:3 * r + 3] * o_win[:, sl])

    psum = p[:, 0:TQ] + p[:, TQ:2 * TQ] + p[:, 2 * TQ:3 * TQ] + p[:, 3 * TQ:4 * TQ]
    ph, plo = _split2(psum)
    ovt = ovt_ref[...]
    imp = _dot(ovt, ph) + _dot(ovt, plo)
    j = lax.broadcasted_iota(I32, (LANE, TQ), 0)
    jt = (t0 + lax.broadcasted_iota(I32, (LANE, TQ), 1)) // SEL_BLOCK
    forced = (j == 0) | (j == jt) | (j == jt - 1)
    valid = j <= jt
    score = jnp.where(valid, imp + jnp.where(forced, SEL_FORCE_BONUS, 0.0), NEG)

    def pick(_, carry):
        sc, sel = carry
        mx = jnp.max(sc, axis=0, keepdims=True)
        idx = jnp.min(jnp.where(sc == mx, j, LANE), axis=0, keepdims=True)
        hit = j == idx
        return jnp.where(hit, -3.0e38, sc), jnp.where(hit, 1.0, sel)

    _, sel = lax.fori_loop(0, SEL_TOPN, pick, (score, jnp.zeros((LANE, TQ), F32)), unroll=True)
    bias_t = jnp.where(valid, jnp.where(sel > 0.0, 0.0, NEG), NEG).astype(BF16)
    for r in range(ATT_GQA):
        qaug_ref[0:LANE, r * TQ:(r + 1) * TQ] = bias_t
    qaug_ref[LANE:KAUG, :] = qst

    scores = functools.partial(_sel_scores, qaug_ref=qaug_ref, kaug_ref=kaug_ref)
    update = functools.partial(_sel_update, vt_ref=vt_ref, acc_ref=acc_ref)
    acc_ref[...] = jnp.zeros(acc_ref.shape, F32)
    kd = pl.multiple_of(t0 + SEL_PAD, TQ)
    kpos = t0 + lax.broadcasted_iota(I32, (TQ, 1), 0)
    m = update(jnp.where(kpos <= t_lane, scores(kd, TQ), NEG), kd, jnp.full((1, R_ROWS), NEG, F32))
    n_t = (t0 + TK - 1) // TK
    kstart = lambda jj: pl.multiple_of(t0 + SEL_PAD - (jj + 1) * TK, TQ)

    @pl.when(n_t > 0)
    def _():
        s0_ref[...] = scores(kstart(0), TK)

    def pair(pp, mm):
        s1_ref[...] = scores(kstart(jnp.minimum(2 * pp + 1, n_t - 1)), TK)
        mm = update(s0_ref[...], kstart(2 * pp), mm)
        s0_ref[...] = scores(kstart(jnp.minimum(2 * pp + 2, n_t - 1)), TK)
        return update(s1_ref[...], kstart(2 * pp + 1), mm)

    m = lax.fori_loop(0, n_t // 2, pair, m)

    @pl.when(n_t % 2 == 1)
    def _():
        update(s0_ref[...], kstart(n_t - 1), m)

    acc = acc_ref[...]
    o_sel = acc[:HEAD_DIM] * (1.0 / acc[HEAD_DIM:HEAD_DIM + 1])

    outs = []
    for r in range(ATT_GQA):
        sl = slice(r * TQ, (r + 1) * TQ)
        outs.append(part_ref[r * HEAD_DIM:(r + 1) * HEAD_DIM, :] + gt[3 * r + 1:3 * r + 2] * o_sel[:, sl])
    o_ref[0] = jnp.concatenate(outs, axis=0).T


def nsa_attention(q, gates, kc, vc, ks, vs, kw, vw):
    b, s, _ = q.shape
    g = N_KV_HEADS
    nc = kc.shape[2]
    n_sel = s // SEL_BLOCK
    n_chunk = s // LANE
    assert n_sel <= LANE and s % TK == 0 and s >= WIN_KEYS
    chunks_t = lambda a: a.reshape(b, g, -1, LANE, a.shape[-1]).swapaxes(3, 4)
    front = lambda a, fill: jnp.concatenate(
        [jnp.broadcast_to(fill, (b, g, SEL_PAD, a.shape[-1])).astype(a.dtype), a], axis=2)
    cs = np.arange(nc)[None, :] * CMP_STRIDE
    js = np.arange(LANE)[:, None] * SEL_BLOCK
    ovt = ((cs < js + SEL_BLOCK) & (cs + CMP_BLOCK > js) & (np.arange(nc)[None, :] < nc - 1)
           & (np.arange(LANE)[:, None] < n_sel))
    ovt = jnp.asarray(ovt, BF16)
    onehot = jnp.asarray(np.arange(s)[:, None] // SEL_BLOCK == np.arange(LANE)[None, :], BF16)
    kaug = jnp.concatenate([jnp.broadcast_to(onehot, (b, g, s, LANE)), ks], axis=-1)
    pad_row = jnp.asarray(np.concatenate([np.ones(LANE), np.zeros(HEAD_DIM)]), BF16)
    kaug = front(kaug, pad_row)
    ones = jnp.broadcast_to(jnp.zeros((s, VROWS - HEAD_DIM), BF16).at[:, 0].set(1), (b, g, s, VROWS - HEAD_DIM))
    vt = chunks_t(front(jnp.concatenate([vs, ones], axis=-1), jnp.zeros((), BF16)))
    vwt = chunks_t(jnp.concatenate([vw, ones], axis=-1))
    tok = np.arange(TQ)[None, :]
    xc = np.arange(2 * nc)[:, None]
    cband = jnp.asarray(np.where(CMP_STRIDE * (xc - nc) + CMP_BLOCK - 1 <= tok, 0.0, NEG), F32)
    xw = np.arange(WINDOW + WIN_KEYS)[:, None]
    wband = jnp.asarray(np.where((xw <= WINDOW + tok) & (xw > tok), 0.0, NEG), F32)
    const = lambda a: pl.BlockSpec(a.shape, lambda bi, gi, i: (0, 0))
    per_bg = lambda *shp: pl.BlockSpec((1, 1) + shp, lambda bi, gi, i: (bi, gi) + (0,) * len(shp))
    return pl.pallas_call(
        _attn_kernel,
        grid=(b, g, s // TQ),
        in_specs=[
            pl.BlockSpec((1, TQ, ATT_GQA * HEAD_DIM), lambda bi, gi, i: (bi, i, gi)),
            pl.BlockSpec((1, 1, 3 * ATT_GQA, TQ), lambda bi, gi, i: (bi, gi, 0, i)),
            per_bg(nc, HEAD_DIM), per_bg(HEAD_DIM, nc),
            const(ovt), const(cband), const(wband),
            per_bg(s + SEL_PAD, KAUG), per_bg(n_chunk + SEL_PAD // LANE, VROWS, LANE),
            per_bg(s, HEAD_DIM), per_bg(n_chunk, VROWS, LANE),
        ],
        out_specs=pl.BlockSpec((1, TQ, ATT_GQA * HEAD_DIM), lambda bi, gi, i: (bi, i, gi)),
        out_shape=jax.ShapeDtypeStruct((b, s, ATT_WIDTH), F32),
        scratch_shapes=[
            pltpu.VMEM((KAUG, R_ROWS), BF16),
            pltpu.VMEM((VROWS, R_ROWS), F32),
            pltpu.VMEM((TK, R_ROWS), F32),
            pltpu.VMEM((TK, R_ROWS), F32),
            pltpu.VMEM((ATT_GQA * HEAD_DIM, TQ), F32),
        ],
        compiler_params=_cparams(("parallel", "parallel", "arbitrary")),
        name="nsa_attention",
    )(q, gates, kc, vc.swapaxes(2, 3), ovt, cband, wband, kaug, vt, kw, vwt)


def _ssd_kernel(xc_ref, xp_ref, z_ref, sm_ref, cw_ref, cb_ref, alog_ref, dsk_ref, nw_ref, o_ref, st_ref):
    c = pl.program_id(1)

    @pl.when(c == 0)
    def _():
        st_ref[...] = jnp.zeros_like(st_ref)

    q = SSM_CHUNK
    xcur = xc_ref[0]
    xprev = xp_ref[0] * (c > 0).astype(F32)
    row = lax.broadcasted_iota(I32, (q, 1), 0)
    cw = cw_ref[...]
    acc = cb_ref[...] + cw[SSM_CONV - 1:SSM_CONV, :] * xcur
    for s in range(1, SSM_CONV):
        shifted = jnp.where(row >= s, pltpu.roll(xcur, s, 0), pltpu.roll(xprev, s, 0))
        acc = acc + cw[SSM_CONV - 1 - s:SSM_CONV - s, :] * shifted
    xc = _silu(acc)
    xs = xc[:, :SSM_WIDTH]
    bm = xc[:, SSM_WIDTH:SSM_WIDTH + 2 * SSM_STATE]
    cm = xc[:, SSM_WIDTH + 2 * SSM_STATE:]

    sm = sm_ref[0]
    lane = lax.broadcasted_iota(I32, (1, LANE), 1)
    head_lane = (lane >= DT_LANE0) & (lane < DT_LANE0 + N_SSM_HEADS)
    da = jnp.where(head_lane, sm * (-jnp.exp(alog_ref[...])), 0.0)
    ri = lax.broadcasted_iota(I32, (q, q), 0)
    ci = lax.broadcasted_iota(I32, (q, q), 1)
    causal = ri >= ci
    tri = jnp.where(causal, 1.0, 0.0).astype(BF16)
    d1, d2, d3 = _split3(da)
    acum = _dot(tri, d1) + _dot(tri, d2) + _dot(tri, d3)
    acum_t = acum.T
    last = acum[q - 1:q, :]
    eac = jnp.exp(acum)
    to_end = jnp.exp(last - acum)
    elast = jnp.exp(last)

    hpg = N_SSM_HEADS // 2
    ys = []
    for g in range(2):
        bg = bm[:, g * SSM_STATE:(g + 1) * SSM_STATE]
        cg = cm[:, g * SSM_STATE:(g + 1) * SSM_STATE]
        gmat = _dot_nt(cg.astype(BF16), bg.astype(BF16))
        bgt = bg.T.astype(BF16)
        for hh in range(hpg):
            h = g * hpg + hh
            ln = DT_LANE0 + h
            x_h = xs[:, h * HEAD_DIM:(h + 1) * HEAD_DIM]
            xdt = x_h * sm[:, ln:ln + 1]
            dec = jnp.exp(jnp.where(causal, acum[:, ln:ln + 1] - acum_t[ln:ln + 1, :], NEG))
            st = st_ref[h]
            y = _dot((gmat * dec).astype(BF16), xdt.astype(BF16)) + _dot(
                (cg * eac[:, ln:ln + 1]).astype(BF16), st.astype(BF16))
            st_ref[h] = elast[:, ln:ln + 1] * st + _dot(bgt, (xdt * to_end[:, ln:ln + 1]).astype(BF16))
            ys.append(y + dsk_ref[:, h * HEAD_DIM:(h + 1) * HEAD_DIM] * x_h)
    y = jnp.concatenate(ys, axis=1)
    yz = y * _silu(z_ref[0])
    half = SSM_WIDTH // 2
    outs = []
    for grp in range(2):
        v = yz[:, grp * half:(grp + 1) * half]
        outs.append(v * lax.rsqrt(jnp.mean(v * v, axis=-1, keepdims=True) + RMS_EPS)
                    * nw_ref[:, grp * half:(grp + 1) * half])
    o_ref[0] = jnp.concatenate(outs, axis=1).astype(BF16)


def ssd_mixer(xbc, z, sm, conv_w, conv_b, a_log, d_skip, ssm_norm_w, b, s):
    q = SSM_CHUNK
    cdim = xbc.shape[-1]
    alog = jnp.zeros((1, LANE), F32).at[0, DT_LANE0:DT_LANE0 + N_SSM_HEADS].set(a_log)
    dsk = jnp.repeat(d_skip, HEAD_DIM)[None, :]
    blk = lambda w: pl.BlockSpec((1, q, w), lambda bi, c: (bi, c, 0))
    full = lambda shp: pl.BlockSpec(shp, lambda bi, c: (0, 0))
    return pl.pallas_call(
        _ssd_kernel,
        grid=(b, s // q),
        in_specs=[
            blk(cdim),
            pl.BlockSpec((1, q, cdim), lambda bi, c: (bi, jnp.maximum(c - 1, 0), 0)),
            blk(SSM_WIDTH), blk(LANE),
            full((SSM_CONV, cdim)), full((1, cdim)), full((1, LANE)), full((1, SSM_WIDTH)), full((1, SSM_WIDTH)),
        ],
        out_specs=blk(SSM_WIDTH),
        out_shape=jax.ShapeDtypeStruct((b, s, SSM_WIDTH), BF16),
        scratch_shapes=[pltpu.VMEM((N_SSM_HEADS, SSM_STATE, HEAD_DIM), F32)],
        compiler_params=_cparams(("parallel", "arbitrary")),
        name="ssd_mixer",
    )(xbc.reshape(b, s, cdim), xbc.reshape(b, s, cdim), z.reshape(b, s, SSM_WIDTH), sm.reshape(b, s, LANE),
      conv_w, conv_b[None, :], alog, dsk, ssm_norm_w[None, :])


def _outproj_kernel(att_ref, ssm_ref, x_ref, mod_ref, anw_ref, wa_ref, ws_ref, lnw_ref, lnb_ref,
                    x1_ref, u2_ref):
    att = att_ref[...]
    an = att * lax.rsqrt(jnp.mean(att * att, axis=-1, keepdims=True) + RMS_EPS) * anw_ref[...]
    h = _dot(an.astype(BF16), wa_ref[...]) + _dot(ssm_ref[...], ws_ref[...])
    gate1 = mod_ref[0, 2:3, :]
    x1 = _ln_rows(DN_ALPHA * x_ref[...] + (1.0 + gate1) * h) * lnw_ref[...] + lnb_ref[...]
    x1_ref[...] = x1
    shift2 = mod_ref[0, 3:4, :]
    scale2 = mod_ref[0, 4:5, :]
    u2_ref[...] = (_ln_rows(x1) * (1.0 + scale2) + shift2).astype(BF16)


def out_proj(att, ssm, x2d, mod, attn_norm_w, w_out, ln_w, ln_b, seq):
    t, d = x2d.shape
    tpb = seq // TM
    wo = w_out.astype(BF16)
    full = lambda shp: pl.BlockSpec(shp, lambda i: (0, 0))
    row = lambda n: pl.BlockSpec((TM, n), lambda i: (i, 0))
    return pl.pallas_call(
        _outproj_kernel,
        grid=(t // TM,),
        in_specs=[
            row(ATT_WIDTH), row(SSM_WIDTH), row(d),
            pl.BlockSpec((1, 6, d), lambda i: (i // tpb, 0, 0)),
            full((1, ATT_WIDTH)), full((ATT_WIDTH, d)), full((SSM_WIDTH, d)), full((1, d)), full((1, d)),
        ],
        out_specs=[row(d), row(d)],
        out_shape=[jax.ShapeDtypeStruct((t, d), F32), jax.ShapeDtypeStruct((t, d), BF16)],
        compiler_params=_cparams(("parallel",)),
        name="out_proj",
    )(att, ssm, x2d, mod, attn_norm_w[None, :], wo[:ATT_WIDTH], wo[ATT_WIDTH:], ln_w[None, :], ln_b[None, :])


def _chunks(n, c):
    return [(f0, min(c, n - f0)) for f0 in range(0, n, c)]


def _ffn_kernel(u_ref, x_ref, mod_ref, wg_ref, wu_ref, wd_ref, lnw_ref, lnb_ref, o_ref, acc_ref):
    u = u_ref[...]
    for k, (f0, fc) in enumerate(_chunks(wg_ref.shape[1], FCHUNK)):
        hid = (_silu(_dot(u, wg_ref[:, f0:f0 + fc])) * _dot(u, wu_ref[:, f0:f0 + fc])).astype(BF16)
        part = _dot(hid, wd_ref[f0:f0 + fc, :])
        if k == 0:
            acc_ref[...] = part
        else:
            acc_ref[...] += part
    gate2 = mod_ref[0, 5:6, :]
    o_ref[...] = _ln_rows(DN_ALPHA * x_ref[...] + (1.0 + gate2) * acc_ref[...]) * lnw_ref[...] + lnb_ref[...]


def dense_ffn(u2, x1, mod, wg, wu, wd, ln_w, ln_b, seq):
    t, d = x1.shape
    f = wg.shape[1]
    tpb = seq // TM
    full = lambda shp: pl.BlockSpec(shp, lambda i: (0, 0))
    row = lambda n: pl.BlockSpec((TM, n), lambda i: (i, 0))
    return pl.pallas_call(
        _ffn_kernel,
        grid=(t // TM,),
        in_specs=[
            row(d), row(d),
            pl.BlockSpec((1, 6, d), lambda i: (i // tpb, 0, 0)),
            full((d, f)), full((d, f)), full((f, d)), full((1, d)), full((1, d)),
        ],
        out_specs=row(d),
        out_shape=jax.ShapeDtypeStruct((t, d), F32),
        scratch_shapes=[pltpu.VMEM((TM, d), F32)],
        compiler_params=_cparams(("parallel",)),
        name="dense_ffn",
    )(u2, x1, mod, wg.astype(BF16), wu.astype(BF16), wd.astype(BF16), ln_w[None, :], ln_b[None, :])


MOE_WIN = MOE_TILE


def _router_kernel(u_ref, rw_ref, g_ref, mt_ref, cnt_ref):
    lt = _dot_nt(rw_ref[...], u_ref[...])
    e = lax.broadcasted_iota(I32, lt.shape, 0)
    top1 = jnp.max(lt, axis=0, keepdims=True)
    m1 = e == jnp.min(jnp.where(lt == top1, e, N_EXPERTS), axis=0, keepdims=True)
    l2 = jnp.where(m1, -3.0e38, lt)
    top2 = jnp.max(l2, axis=0, keepdims=True)
    m2 = e == jnp.min(jnp.where(l2 == top2, e, N_EXPERTS), axis=0, keepdims=True)
    e2 = jnp.exp(top2 - top1)
    den = 1.0 + e2
    gates = jnp.where(m1, 1.0 / den, 0.0) + jnp.where(m2, e2 / den, 0.0)
    member = jnp.where(m1 | m2, 1.0, 0.0)
    pad = jnp.zeros((LANE - 2 * N_EXPERTS, lt.shape[1]), F32)
    g_ref[...] = jnp.concatenate([gates, member, pad], axis=0).T
    mt_ref[...] = member
    cnt_ref[0] = jnp.broadcast_to(jnp.sum(member, axis=1, keepdims=True), (N_EXPERTS, LANE))


def moe_router(u2, router_w):
    t, d = u2.shape
    nt = t // MOE_TILE
    return pl.pallas_call(
        _router_kernel,
        grid=(nt,),
        in_specs=[pl.BlockSpec((MOE_TILE, d), lambda i: (i, 0)), pl.BlockSpec((N_EXPERTS, d), lambda i: (0, 0))],
        out_specs=[
            pl.BlockSpec((MOE_TILE, LANE), lambda i: (i, 0)),
            pl.BlockSpec((N_EXPERTS, MOE_TILE), lambda i: (0, i)),
            pl.BlockSpec((1, N_EXPERTS, LANE), lambda i: (i, 0, 0)),
        ],
        out_shape=[
            jax.ShapeDtypeStruct((t, LANE), F32),
            jax.ShapeDtypeStruct((N_EXPERTS, t), F32),
            jax.ShapeDtypeStruct((nt, N_EXPERTS, LANE), F32),
        ],
        compiler_params=_cparams(("parallel",)),
        name="moe_router",
    )(u2, router_w.T.astype(BF16))


def _dispatch_copies(base_ref, buf_ref, xs_ref, sem_ref, step, slot):
    return [
        pltpu.make_async_copy(
            buf_ref.at[slot, e],
            xs_ref.at[pl.ds(pl.multiple_of(base_ref[step * N_EXPERTS + e], MOE_ALIGN), MOE_WIN)],
            sem_ref.at[slot])
        for e in range(N_EXPERTS)
    ]


def _dispatch_kernel(base_ref, u_ref, mt_ref, xs_in_ref, xs_ref, buf_ref, sem_ref):
    del xs_in_ref
    i = pl.program_id(0)
    slot = i % 2
    mt = mt_ref[...]
    n = mt.shape[1]
    ri = lax.broadcasted_iota(I32, (n, n), 0)
    ci = lax.broadcasted_iota(I32, (n, n), 1)
    before = jnp.where(ri < ci, 1.0, 0.0).astype(BF16)
    rank = _dot(mt.astype(BF16), before).astype(I32)
    u = u_ref[...]
    wr = lax.broadcasted_iota(I32, (MOE_WIN, n), 0)
    for e in range(N_EXPERTS):
        pick = jnp.where((wr == rank[e:e + 1, :]) & (mt[e:e + 1, :] > 0.0), 1.0, 0.0).astype(BF16)
        buf_ref[slot, e] = _dot(pick, u).astype(BF16)

    @pl.when(i > 0)
    def _():
        for cp in _dispatch_copies(base_ref, buf_ref, xs_ref, sem_ref, i - 1, 1 - slot):
            cp.wait()

    for cp in _dispatch_copies(base_ref, buf_ref, xs_ref, sem_ref, i, slot):
        cp.start()

    @pl.when(i == pl.num_programs(0) - 1)
    def _():
        for cp in _dispatch_copies(base_ref, buf_ref, xs_ref, sem_ref, i, slot):
            cp.wait()


def moe_dispatch(u2, member_t, base, n_rows):
    t, d = u2.shape
    nt = t // MOE_TILE
    xs0 = jnp.zeros((n_rows, d), BF16)
    grid_spec = pltpu.PrefetchScalarGridSpec(
        num_scalar_prefetch=1,
        grid=(nt,),
        in_specs=[
            pl.BlockSpec((MOE_TILE, d), lambda i, base: (i, 0)),
            pl.BlockSpec((N_EXPERTS, MOE_TILE), lambda i, base: (0, i)),
            pl.BlockSpec(memory_space=pl.ANY),
        ],
        out_specs=pl.BlockSpec(memory_space=pl.ANY),
        scratch_shapes=[
            pltpu.VMEM((2, N_EXPERTS, MOE_WIN, d), BF16),
            pltpu.SemaphoreType.DMA((2,)),
        ],
    )
    return pl.pallas_call(
        _dispatch_kernel,
        grid_spec=grid_spec,
        out_shape=jax.ShapeDtypeStruct((n_rows, d), BF16),
        input_output_aliases={3: 0},
        compiler_params=_cparams(("arbitrary",)),
        name="moe_dispatch",
    )(base.reshape(-1), u2, member_t, xs0)


def _expert_kernel(be_ref, act_ref, x_ref, wg_ref, wu_ref, wd_ref, y_ref, acc_ref):
    i = pl.program_id(0)

    @pl.when(act_ref[i] > 0)
    def _():
        x = x_ref[...]
        for k, (f0, fc) in enumerate(_chunks(wg_ref.shape[2], FCHUNK)):
            hid = (_silu(_dot(x, wg_ref[0, :, f0:f0 + fc])) * _dot(x, wu_ref[0, :, f0:f0 + fc])).astype(BF16)
            part = _dot(hid, wd_ref[0, f0:f0 + fc, :])
            if k == 0:
                acc_ref[...] = part
            else:
                acc_ref[...] += part
        y_ref[...] = acc_ref[...].astype(BF16)

    @pl.when(act_ref[i] == 0)
    def _():
        y_ref[...] = jnp.zeros_like(y_ref)


def moe_experts(xs, blk_expert, blk_active, wg, wu, wd):
    n_rows, d = xs.shape
    f = wg.shape[2]
    resident = dict(pipeline_mode=pl.Buffered(1))
    grid_spec = pltpu.PrefetchScalarGridSpec(
        num_scalar_prefetch=2,
        grid=(n_rows // MOE_BLK,),
        in_specs=[
            pl.BlockSpec((MOE_BLK, d), lambda i, be, act: (i, 0)),
            pl.BlockSpec((1, d, f), lambda i, be, act: (be[i], 0, 0), **resident),
            pl.BlockSpec((1, d, f), lambda i, be, act: (be[i], 0, 0), **resident),
            pl.BlockSpec((1, f, d), lambda i, be, act: (be[i], 0, 0), **resident),
        ],
        out_specs=pl.BlockSpec((MOE_BLK, d), lambda i, be, act: (i, 0)),
        scratch_shapes=[pltpu.VMEM((MOE_BLK, d), F32)],
    )
    return pl.pallas_call(
        _expert_kernel,
        grid_spec=grid_spec,
        out_shape=jax.ShapeDtypeStruct((n_rows, d), BF16),
        compiler_params=_cparams(("arbitrary",)),
        name="moe_experts",
    )(blk_expert, blk_active, xs, wg.astype(BF16), wu.astype(BF16), wd.astype(BF16))


def _combine_copies(base_ref, y_ref, buf_ref, sem_ref, step, slot):
    return [
        pltpu.make_async_copy(
            y_ref.at[pl.ds(pl.multiple_of(base_ref[step * N_EXPERTS + e], MOE_ALIGN), MOE_WIN)],
            buf_ref.at[slot, e],
            sem_ref.at[slot])
        for e in range(N_EXPERTS)
    ]


def _combine_kernel(base_ref, g_ref, x_ref, mod_ref, lnw_ref, lnb_ref, y_ref, o_ref, buf_ref, sem_ref):
    i = pl.program_id(0)
    slot = i % 2

    @pl.when(i == 0)
    def _():
        for cp in _combine_copies(base_ref, y_ref, buf_ref, sem_ref, 0, 0):
            cp.start()

    @pl.when(i + 1 < pl.num_programs(0))
    def _():
        for cp in _combine_copies(base_ref, y_ref, buf_ref, sem_ref, i + 1, 1 - slot):
            cp.start()

    gm = g_ref[...]
    n = gm.shape[0]
    ri = lax.broadcasted_iota(I32, (n, n), 0)
    ci = lax.broadcasted_iota(I32, (n, n), 1)
    before = jnp.where(ci < ri, 1.0, 0.0).astype(BF16)
    lane = lax.broadcasted_iota(I32, (1, LANE), 1)
    member = jnp.where((lane >= N_EXPERTS) & (lane < 2 * N_EXPERTS), gm, 0.0)
    rank = _dot(before, member.astype(BF16)).astype(I32)
    wc = lax.broadcasted_iota(I32, (n, MOE_WIN), 1)

    for cp in _combine_copies(base_ref, y_ref, buf_ref, sem_ref, i, slot):
        cp.wait()

    f = jnp.zeros((n, x_ref.shape[1]), F32)
    for e in range(N_EXPERTS):
        le = N_EXPERTS + e
        pick = jnp.where((wc == rank[:, le:le + 1]) & (gm[:, le:le + 1] > 0.0), 1.0, 0.0).astype(BF16)
        f = f + gm[:, e:e + 1] * _dot(pick, buf_ref[slot, e])
    gate2 = mod_ref[0, 5:6, :]
    o_ref[...] = _ln_rows(DN_ALPHA * x_ref[...] + (1.0 + gate2) * f) * lnw_ref[...] + lnb_ref[...]


def moe_combine(y, gates, base, x1, mod, ln_w, ln_b, seq):
    t, d = x1.shape
    nt = t // MOE_TILE
    tpb = seq // MOE_TILE
    grid_spec = pltpu.PrefetchScalarGridSpec(
        num_scalar_prefetch=1,
        grid=(nt,),
        in_specs=[
            pl.BlockSpec((MOE_TILE, LANE), lambda i, base: (i, 0)),
            pl.BlockSpec((MOE_TILE, d), lambda i, base: (i, 0)),
            pl.BlockSpec((1, 6, d), lambda i, base: (i // tpb, 0, 0)),
            pl.BlockSpec((1, d), lambda i, base: (0, 0)),
            pl.BlockSpec((1, d), lambda i, base: (0, 0)),
            pl.BlockSpec(memory_space=pl.ANY),
        ],
        out_specs=pl.BlockSpec((MOE_TILE, d), lambda i, base: (i, 0)),
        scratch_shapes=[
            pltpu.VMEM((2, N_EXPERTS, MOE_WIN, d), BF16),
            pltpu.SemaphoreType.DMA((2,)),
        ],
    )
    return pl.pallas_call(
        _combine_kernel,
        grid_spec=grid_spec,
        out_shape=jax.ShapeDtypeStruct((t, d), F32),
        compiler_params=_cparams(("arbitrary",)),
        name="moe_combine",
    )(base.reshape(-1), gates, x1, mod, ln_w[None, :], ln_b[None, :], y)


def moe_ffn(u2, x1, mod, router_w, wg, wu, wd, ln_w, ln_b, seq):
    t, d = u2.shape
    nt = t // MOE_TILE
    gates, member_t, cnt = moe_router(u2, router_w)
    cnt = cnt[:, :, 0].astype(I32)
    seg = (cnt + MOE_ALIGN - 1) // MOE_ALIGN * MOE_ALIGN
    region = (jnp.sum(seg, axis=0) + MOE_WIN + MOE_BLK - 1) // MOE_BLK * MOE_BLK
    region_end = jnp.cumsum(region)
    base = (region_end - region)[None, :] + jnp.cumsum(seg, axis=0) - seg
    n_blk = -(-(2 * t + nt * N_EXPERTS * (MOE_ALIGN - 1) + N_EXPERTS * (MOE_WIN + MOE_BLK - 1)) // MOE_BLK)
    blk_start = jnp.arange(n_blk, dtype=I32) * MOE_BLK
    blk_expert = jnp.minimum(jnp.sum((blk_start[:, None] >= region_end[None, :]).astype(I32), axis=1), N_EXPERTS - 1)
    blk_active = (blk_start < region_end[-1]).astype(I32)
    xs = moe_dispatch(u2, member_t, base.astype(I32), n_blk * MOE_BLK)
    y = moe_experts(xs, blk_expert, blk_active, wg, wu, wd)
    return moe_combine(y, gates, base.astype(I32), x1, mod, ln_w, ln_b, seq)


def _mixer_front(x, mod, w_in, dt_bias, cmp_pos_k, cmp_w1_k, cmp_w2_k, cmp_pos_v, cmp_w1_v, cmp_w2_v):
    b, s, d = x.shape
    t = b * s
    g = N_KV_HEADS
    q, kv, z, xbc, sm = in_proj(x.reshape(t, d), mod, w_in, dt_bias, s)
    kv = kv.reshape(b, s, 6, 128)
    kvc = compress(jnp.stack([kv[:, :, 0], kv[:, :, 1]]),
                   jnp.stack([cmp_pos_k, cmp_pos_v]), jnp.stack([cmp_w1_k, cmp_w1_v]),
                   jnp.stack([cmp_w2_k, cmp_w2_v]))
    heads = lambda a: a.reshape(b, -1, g, HEAD_DIM).transpose(0, 2, 1, 3)
    gates = sm[:, :GATE_COLS].reshape(b, s, g, 3 * ATT_GQA).transpose(0, 2, 3, 1)
    att = nsa_attention(q.reshape(b, s, ATT_WIDTH), gates, heads(kvc[0]), heads(kvc[1]),
                        heads(kv[:, :, 2]), heads(kv[:, :, 3]), heads(kv[:, :, 4]), heads(kv[:, :, 5]))
    return att, z, xbc, sm


def kernel(x, c, ada_w, ada_b, w_in, cmp_pos_k, cmp_w1_k, cmp_w2_k, cmp_pos_v, cmp_w1_v, cmp_w2_v, attn_norm_w, conv_w, conv_b, dt_bias, a_log, d_skip, ssm_norm_w, w_out, ln1_w, ln1_b, ln2_w, ln2_b, ffn_w_gate, ffn_w_up, ffn_w_down, router_w, exp_w_gate, exp_w_up, exp_w_down):
    b, s, d = x.shape
    t = b * s
    mod = ada_mod(c, ada_w, ada_b).reshape(DEPTH, b, 6, d)
    xc = x.reshape(t, d)
    for l in range(DEPTH):
        att, z, xbc, sm = _mixer_front(xc.reshape(b, s, d), mod[l], w_in[l], dt_bias[l],
                                       cmp_pos_k[l], cmp_w1_k[l], cmp_w2_k[l],
                                       cmp_pos_v[l], cmp_w1_v[l], cmp_w2_v[l])
        ssm = ssd_mixer(xbc, z, sm, conv_w[l], conv_b[l], a_log[l], d_skip[l], ssm_norm_w[l], b, s)
        x1, u2 = out_proj(att.reshape(t, ATT_WIDTH), ssm.reshape(t, SSM_WIDTH), xc, mod[l], attn_norm_w[l],
                          w_out[l], ln1_w[l], ln1_b[l], s)
        i = l // 2
        if l % 2 == 0:
            xc = dense_ffn(u2, x1, mod[l], ffn_w_gate[i], ffn_w_up[i], ffn_w_down[i], ln2_w[l], ln2_b[l], s)
        else:
            xc = moe_ffn(u2, x1, mod[l], router_w[i], exp_w_gate[i], exp_w_up[i], exp_w_down[i],
                         ln2_w[l], ln2_b[l], s)
    return xc.reshape(b, s, d)
```

```python
import functools
import math

import numpy as np
import jax
import jax.numpy as jnp
from jax import lax
from jax.experimental import pallas as pl
from jax.experimental.pallas import tpu as pltpu

F32 = jnp.float32
BF16 = jnp.bfloat16
I32 = jnp.int32

D_MODEL = 1024
HEAD_DIM = 64
ATT_WIDTH = 512
N_KV_HEADS = 2
ATT_GQA = 4
CMP_BLOCK = 32
CMP_STRIDE = 16
CMP_HIDDEN = 256
SEL_BLOCK = 64
SEL_TOPN = 16
SEL_FORCE_BONUS = 1e4
WINDOW = 512
SSM_WIDTH = 512
N_SSM_HEADS = 8
SSM_STATE = 128
SSM_CONV = 4
SSM_CHUNK = 128
FFN_DENSE = 2816
N_EXPERTS = 8
FFN_EXPERT = 3584
DEPTH = 2
DN_ALPHA = (2 * DEPTH) ** 0.25
LN_EPS = 1e-5
RMS_EPS = 1e-6
NEG = -1e30

LANE = 128
VMEM_LIMIT = 56 * 1024 * 1024

TM = 512
TQ = 256
TK = 512
MOE_TILE = 512
MOE_ALIGN = 16
MOE_BLK = 1024
FCHUNK = 512


def _cparams(sem):
    return pltpu.CompilerParams(dimension_semantics=sem, vmem_limit_bytes=VMEM_LIMIT)


def _dot(a, b):
    return jnp.dot(a, b, preferred_element_type=F32)


def _dot_nt(a, b):
    return lax.dot_general(a, b, (((1,), (1,)), ((), ())), preferred_element_type=F32)


def _split2(x):
    hi = x.astype(BF16)
    lo = (x - hi.astype(F32)).astype(BF16)
    return hi, lo


def _split3(x):
    hi = x.astype(BF16)
    r = x - hi.astype(F32)
    mid = r.astype(BF16)
    lo = (r - mid.astype(F32)).astype(BF16)
    return hi, mid, lo


def _sigmoid(x):
    return 1.0 / (1.0 + jnp.exp(-x))


def _silu(x):
    return x * _sigmoid(x)


def _ln_rows(x):
    mu = jnp.mean(x, axis=-1, keepdims=True)
    xc = x - mu
    var = jnp.mean(xc * xc, axis=-1, keepdims=True)
    return xc * lax.rsqrt(var + LN_EPS)


def _ada_kernel(c_ref, w_ref, b_ref, o_ref):
    cond = _silu(c_ref[...])
    ch, cl = _split2(cond)
    wh, wl = _split2(w_ref[0])
    o_ref[0] = _dot(ch, wh) + _dot(cl, wh) + _dot(ch, wl) + b_ref[0]


def ada_mod(c, ada_w, ada_b):
    nl, d, n = ada_w.shape
    b = c.shape[0]
    tn = 1536
    return pl.pallas_call(
        _ada_kernel,
        grid=(nl, n // tn),
        in_specs=[
            pl.BlockSpec((b, d), lambda l, j: (0, 0)),
            pl.BlockSpec((1, d, tn), lambda l, j: (l, 0, j)),
            pl.BlockSpec((1, 1, tn), lambda l, j: (l, 0, j)),
        ],
        out_specs=pl.BlockSpec((1, b, tn), lambda l, j: (l, 0, j)),
        out_shape=jax.ShapeDtypeStruct((nl, b, n), F32),
        compiler_params=_cparams(("parallel", "parallel")),
        name="ada_mod",
    )(c, ada_w, ada_b.reshape(nl, 1, n))


KV_COLS = 6 * 128
GATE_COLS = 24
DT_LANE0 = 32


def _inproj_kernel(x_ref, mod_ref, wq_ref, wkv_ref, wz_ref, wx_ref, wsh_ref, wsl_ref, dtb_ref,
                   q_ref, kv_ref, z_ref, xbc_ref, sm_ref):
    xn = _ln_rows(x_ref[...])
    shift = mod_ref[0, 0:1, :]
    scale = mod_ref[0, 1:2, :]
    u = xn * (1.0 + scale) + shift
    ub = u.astype(BF16)
    ul = (u - ub.astype(F32)).astype(BF16)
    q_ref[...] = (_dot(ub, wq_ref[...]) * (LOG2E * HEAD_DIM ** -0.5)).astype(BF16)
    kv_ref[...] = _dot(ub, wkv_ref[...]).astype(BF16)
    z_ref[...] = _dot(ub, wz_ref[...])
    xbc_ref[...] = _dot(ub, wx_ref[...])
    sm = _dot(ub, wsh_ref[...]) + _dot(ul, wsh_ref[...]) + _dot(ub, wsl_ref[...])
    lane = lax.broadcasted_iota(I32, sm.shape, 1)
    xdt = sm + dtb_ref[...]
    dt = jnp.maximum(xdt, 0.0) + jnp.log(1.0 + jnp.exp(-jnp.abs(xdt)))
    sm_ref[...] = jnp.where(lane < GATE_COLS, _sigmoid(sm), dt)


def in_proj(x2d, mod, w_in, dt_bias, seq):
    t, d = x2d.shape
    wq = w_in[:, :512].astype(BF16)
    wkv = w_in[:, 512:512 + KV_COLS].astype(BF16)
    o = 512 + KV_COLS
    wg = w_in[:, o:o + GATE_COLS]
    wz = w_in[:, o + GATE_COLS:o + GATE_COLS + 512].astype(BF16)
    o2 = o + GATE_COLS + 512
    wx = w_in[:, o2:o2 + 1024].astype(BF16)
    wdt = w_in[:, o2 + 1024:]
    wsm = jnp.zeros((d, LANE), F32).at[:, :GATE_COLS].set(wg).at[:, DT_LANE0:DT_LANE0 + N_SSM_HEADS].set(wdt)
    wsh, wsl = _split2(wsm)
    dtb = jnp.zeros((1, LANE), F32).at[0, DT_LANE0:DT_LANE0 + N_SSM_HEADS].set(dt_bias)
    tpb = seq // TM
    full = lambda shp: pl.BlockSpec(shp, lambda i: (0, 0))
    row = lambda n: pl.BlockSpec((TM, n), lambda i: (i, 0))
    return pl.pallas_call(
        _inproj_kernel,
        grid=(t // TM,),
        in_specs=[
            row(d),
            pl.BlockSpec((1, 6, d), lambda i: (i // tpb, 0, 0)),
            full((d, 512)), full((d, KV_COLS)), full((d, 512)), full((d, 1024)),
            full((d, LANE)), full((d, LANE)), full((1, LANE)),
        ],
        out_specs=[row(512), row(KV_COLS), row(512), row(1024), row(LANE)],
        out_shape=[
            jax.ShapeDtypeStruct((t, 512), BF16),
            jax.ShapeDtypeStruct((t, KV_COLS), BF16),
            jax.ShapeDtypeStruct((t, 512), F32),
            jax.ShapeDtypeStruct((t, 1024), F32),
            jax.ShapeDtypeStruct((t, LANE), F32),
        ],
        compiler_params=_cparams(("parallel",)),
        name="in_proj",
    )(x2d, mod, wq, wkv, wz, wx, wsh, wsl, dtb)


def _gelu_tanh(x):
    return 0.5 * x * (1.0 + jnp.tanh(math.sqrt(2.0 / math.pi) * (x + 0.044715 * (x * x * x))))


def _cmp_kernel(x_ref, w1e_ref, w1_ref, pos_ref, w2e_ref, o_ref):
    x = x_ref[0, 0]
    nc = x.shape[0]
    ab = _dot(x, w1e_ref[0])
    half = N_KV_HEADS * CMP_HIDDEN
    first = ab[:, :half]
    second = pltpu.roll(ab[:, half:], nc - 1, 0)
    ph, plo = _split2(pos_ref[0])
    pb = (_dot(ph, w1_ref[0]) + _dot(plo, w1_ref[0]))[0:1, :]
    pb = jnp.concatenate([pb] * N_KV_HEADS, axis=1)
    h = _gelu_tanh(first + second + pb)
    o_ref[0, 0] = _dot(h.astype(BF16), w2e_ref[0]).astype(BF16)


def compress(kv_raw, pos, w1, w2):
    _, b, s, _ = kv_raw.shape
    nc = s // CMP_STRIDE
    x = kv_raw.reshape(2, b, nc, CMP_STRIDE * 128)
    g = N_KV_HEADS
    w1r = w1.reshape(2, 2, CMP_STRIDE, HEAD_DIM, CMP_HIDDEN)
    eye = jnp.eye(g, dtype=F32)
    w1e = jnp.einsum('kjtdh,ga->ktgdjah', w1r, eye).reshape(2, CMP_STRIDE * g * HEAD_DIM, 2 * g * CMP_HIDDEN)
    w2e = jnp.einsum('khd,ga->kghad', w2, eye).reshape(2, g * CMP_HIDDEN, g * HEAD_DIM)
    posf = jnp.zeros((2, 8, CMP_BLOCK * HEAD_DIM), F32).at[:, 0, :].set(pos.reshape(2, -1))
    out = pl.pallas_call(
        _cmp_kernel,
        grid=(2, b),
        in_specs=[
            pl.BlockSpec((1, 1, nc, CMP_STRIDE * 128), lambda k, i: (k, i, 0, 0)),
            pl.BlockSpec((1,) + w1e.shape[1:], lambda k, i: (k, 0, 0)),
            pl.BlockSpec((1,) + w1.shape[1:], lambda k, i: (k, 0, 0)),
            pl.BlockSpec((1,) + posf.shape[1:], lambda k, i: (k, 0, 0)),
            pl.BlockSpec((1,) + w2e.shape[1:], lambda k, i: (k, 0, 0)),
        ],
        out_specs=pl.BlockSpec((1, 1, nc, 128), lambda k, i: (k, i, 0, 0)),
        out_shape=jax.ShapeDtypeStruct((2, b, nc, 128), BF16),
        compiler_params=_cparams(("parallel", "parallel")),
        name="compress",
    )(x, w1e.astype(BF16), w1.astype(BF16), posf, w2e.astype(BF16))
    return out


R_ROWS = ATT_GQA * TQ
WIN_KEYS = WINDOW + TQ
KAUG = LANE + HEAD_DIM
VROWS = HEAD_DIM + 16
LOG2E = math.log2(math.e)
SEL_PAD = TK - TQ


def _exp_cols(s):
    return jnp.exp2(s - jnp.max(s, axis=0, keepdims=True))


def _sel_scores(k0, width, qaug_ref, kaug_ref):
    return _dot(kaug_ref[0, 0, pl.ds(k0, width), :], qaug_ref[...])


def _sel_update(st, k0, m_old, vt_ref, acc_ref):
    m_new = jnp.maximum(m_old, jnp.max(st, axis=0, keepdims=True))
    pt = jnp.exp2(st - m_new).astype(BF16)
    c0 = k0 // LANE
    vt = jnp.concatenate([vt_ref[0, 0, c0 + c] for c in range(st.shape[0] // LANE)], axis=1)
    acc_ref[...] = jnp.exp2(m_old - m_new) * acc_ref[...] + _dot(vt, pt)
    return m_new


def _attn_kernel(q_ref, gate_ref, kc_ref, vct_ref, ovt_ref, cband_ref, wband_ref, kaug_ref, vt_ref, kw_ref, vwt_ref,
                 o_ref, qaug_ref, acc_ref, s0_ref, s1_ref, part_ref):
    i = pl.program_id(2)
    t0 = i * TQ
    qt = q_ref[0].astype(F32).T.astype(BF16)
    qst = jnp.concatenate([qt[r * HEAD_DIM:(r + 1) * HEAD_DIM, :] for r in range(ATT_GQA)], axis=1)
    t_lane = t0 + lax.broadcasted_iota(I32, (1, R_ROWS), 1) % TQ

    kc = kc_ref[0, 0]
    nc = kc.shape[0]
    cband = cband_ref[pl.ds(pl.multiple_of(nc - t0 // CMP_STRIDE, 8), nc), :]
    e = _exp_cols(_dot(kc, qst) + jnp.concatenate([cband] * ATT_GQA, axis=1))
    p = e * jnp.where(t_lane >= CMP_BLOCK - 1, 1.0 / jnp.sum(e, axis=0, keepdims=True), 0.0)
    o_cmp = _dot(vct_ref[0, 0], p.astype(BF16))

    ws = pl.multiple_of(jnp.maximum(t0 - WINDOW, 0), TQ)
    wband = wband_ref[pl.ds(pl.multiple_of(WINDOW - (t0 - ws), TQ), WIN_KEYS), :]
    ew = _exp_cols(_dot(kw_ref[0, 0, pl.ds(ws, WIN_KEYS), :], qst)
                   + jnp.concatenate([wband] * ATT_GQA, axis=1)).astype(BF16)
    cw = ws // LANE
    o_win = _dot(vwt_ref[0, 0, cw], ew[0:LANE])
    for c in range(1, WIN_KEYS // LANE):
        o_win = o_win + _dot(vwt_ref[0, 0, cw + c], ew[c * LANE:(c + 1) * LANE])
    o_win = o_win[:HEAD_DIM] * (1.0 / o_win[HEAD_DIM:HEAD_DIM + 1])

    gt = gate_ref[0, 0]
    for r in range(ATT_GQA):
        sl = slice(r * TQ, (r + 1) * TQ)
        part_ref[r * HEAD_DIM:(r + 1) * HEAD_DIM, :] = (gt[3 * r:3 * r + 1] * o_cmp[:, sl]
                                                        + gt[3 * r + 2:3 * r + 3] * o_win[:, sl])

    psum = p[:, 0:TQ] + p[:, TQ:2 * TQ] + p[:, 2 * TQ:3 * TQ] + p[:, 3 * TQ:4 * TQ]
    ph, plo = _split2(psum)
    ovt = ovt_ref[...]
    imp = _dot(ovt, ph) + _dot(ovt, plo)
    j = lax.broadcasted_iota(I32, (LANE, TQ), 0)
    jt = (t0 + lax.broadcasted_iota(I32, (LANE, TQ), 1)) // SEL_BLOCK
    forced = (j == 0) | (j == jt) | (j == jt - 1)
    valid = j <= jt
    score = jnp.where(forced, NEG, jnp.where(valid, imp, NEG))

    def pick(_, carry):
        sc, sel = carry
        mx = jnp.max(sc, axis=0, keepdims=True)
        idx = jnp.min(jnp.where(sc == mx, j, LANE), axis=0, keepdims=True)
        hit = j == idx
        return jnp.where(hit, -3.0e38, sc), jnp.where(hit, 1.0, sel)

    _, sel = lax.fori_loop(0, SEL_TOPN - 3, pick, (score, jnp.where(forced, 1.0, 0.0)), unroll=True)
    bias_t = jnp.where(valid, jnp.where(sel > 0.0, 0.0, NEG), NEG).astype(BF16)
    for r in range(ATT_GQA):
        qaug_ref[0:LANE, r * TQ:(r + 1) * TQ] = bias_t
    qaug_ref[LANE:KAUG, :] = qst

    scores = functools.partial(_sel_scores, qaug_ref=qaug_ref, kaug_ref=kaug_ref)
    update = functools.partial(_sel_update, vt_ref=vt_ref, acc_ref=acc_ref)
    acc_ref[...] = jnp.zeros(acc_ref.shape, F32)
    kd = pl.multiple_of(t0 + SEL_PAD, TQ)
    kpos = t0 + lax.broadcasted_iota(I32, (TQ, 1), 0)
    m = update(jnp.where(kpos <= t_lane, scores(kd, TQ), NEG), kd, jnp.full((1, R_ROWS), NEG, F32))
    n_t = (t0 + TK - 1) // TK
    kstart = lambda jj: pl.multiple_of(t0 + SEL_PAD - (jj + 1) * TK, TQ)

    @pl.when(n_t > 0)
    def _():
        s0_ref[...] = scores(kstart(0), TK)

    def pair(pp, mm):
        s1_ref[...] = scores(kstart(jnp.minimum(2 * pp + 1, n_t - 1)), TK)
        mm = update(s0_ref[...], kstart(2 * pp), mm)
        s0_ref[...] = scores(kstart(jnp.minimum(2 * pp + 2, n_t - 1)), TK)
        return update(s1_ref[...], kstart(2 * pp + 1), mm)

    m = lax.fori_loop(0, n_t // 2, pair, m)

    @pl.when(n_t % 2 == 1)
    def _():
        update(s0_ref[...], kstart(n_t - 1), m)

    acc = acc_ref[...]
    o_sel = acc[:HEAD_DIM] * (1.0 / acc[HEAD_DIM:HEAD_DIM + 1])

    outs = []
    for r in range(ATT_GQA):
        sl = slice(r * TQ, (r + 1) * TQ)
        outs.append(part_ref[r * HEAD_DIM:(r + 1) * HEAD_DIM, :] + gt[3 * r + 1:3 * r + 2] * o_sel[:, sl])
    o_ref[0] = jnp.concatenate(outs, axis=0).T


def nsa_attention(q, gates, kc, vc, ks, vs, kw, vw):
    b, s, _ = q.shape
    g = N_KV_HEADS
    nc = kc.shape[2]
    n_sel = s // SEL_BLOCK
    n_chunk = s // LANE
    assert SEL_TOPN <= n_sel <= LANE and s % TK == 0 and s >= WIN_KEYS
    chunks_t = lambda a: a.reshape(b, g, -1, LANE, a.shape[-1]).swapaxes(3, 4)
    front = lambda a, fill: jnp.concatenate(
        [jnp.broadcast_to(fill, (b, g, SEL_PAD, a.shape[-1])).astype(a.dtype), a], axis=2)
    cs = np.arange(nc)[None, :] * CMP_STRIDE
    js = np.arange(LANE)[:, None] * SEL_BLOCK
    ovt = ((cs < js + SEL_BLOCK) & (cs + CMP_BLOCK > js) & (np.arange(nc)[None, :] < nc - 1)
           & (np.arange(LANE)[:, None] < n_sel))
    ovt = jnp.asarray(ovt, BF16)
    onehot = jnp.asarray(np.arange(s)[:, None] // SEL_BLOCK == np.arange(LANE)[None, :], BF16)
    kaug = jnp.concatenate([jnp.broadcast_to(onehot, (b, g, s, LANE)), ks], axis=-1)
    pad_row = jnp.asarray(np.concatenate([np.ones(LANE), np.zeros(HEAD_DIM)]), BF16)
    kaug = front(kaug, pad_row)
    ones = jnp.broadcast_to(jnp.zeros((s, VROWS - HEAD_DIM), BF16).at[:, 0].set(1), (b, g, s, VROWS - HEAD_DIM))
    vt = chunks_t(front(jnp.concatenate([vs, ones], axis=-1), jnp.zeros((), BF16)))
    vwt = chunks_t(jnp.concatenate([vw, ones], axis=-1))
    tok = np.arange(TQ)[None, :]
    xc = np.arange(2 * nc)[:, None]
    cband = jnp.asarray(np.where(CMP_STRIDE * (xc - nc) + CMP_BLOCK - 1 <= tok, 0.0, NEG), F32)
    xw = np.arange(WINDOW + WIN_KEYS)[:, None]
    wband = jnp.asarray(np.where((xw <= WINDOW + tok) & (xw > tok), 0.0, NEG), F32)
    const = lambda a: pl.BlockSpec(a.shape, lambda bi, gi, i: (0, 0))
    per_bg = lambda *shp: pl.BlockSpec((1, 1) + shp, lambda bi, gi, i: (bi, gi) + (0,) * len(shp))
    return pl.pallas_call(
        _attn_kernel,
        grid=(b, g, s // TQ),
        in_specs=[
            pl.BlockSpec((1, TQ, ATT_GQA * HEAD_DIM), lambda bi, gi, i: (bi, i, gi)),
            pl.BlockSpec((1, 1, 3 * ATT_GQA, TQ), lambda bi, gi, i: (bi, gi, 0, i)),
            per_bg(nc, HEAD_DIM), per_bg(HEAD_DIM, nc),
            const(ovt), const(cband), const(wband),
            per_bg(s + SEL_PAD, KAUG), per_bg(n_chunk + SEL_PAD // LANE, VROWS, LANE),
            per_bg(s, HEAD_DIM), per_bg(n_chunk, VROWS, LANE),
        ],
        out_specs=pl.BlockSpec((1, TQ, ATT_GQA * HEAD_DIM), lambda bi, gi, i: (bi, i, gi)),
        out_shape=jax.ShapeDtypeStruct((b, s, ATT_WIDTH), F32),
        scratch_shapes=[
            pltpu.VMEM((KAUG, R_ROWS), BF16),
            pltpu.VMEM((VROWS, R_ROWS), F32),
            pltpu.VMEM((TK, R_ROWS), F32),
            pltpu.VMEM((TK, R_ROWS), F32),
            pltpu.VMEM((ATT_GQA * HEAD_DIM, TQ), F32),
        ],
        compiler_params=_cparams(("parallel", "parallel", "arbitrary")),
        name="nsa_attention",
    )(q, gates, kc, vc.swapaxes(2, 3), ovt, cband, wband, kaug, vt, kw, vwt)


def _ssd_kernel(xc_ref, xp_ref, z_ref, sm_ref, cw_ref, cb_ref, alog_ref, dsk_ref, nw_ref, exp_ref, o_ref, st_ref):
    c = pl.program_id(1)

    @pl.when(c == 0)
    def _():
        st_ref[...] = jnp.zeros_like(st_ref)

    q = SSM_CHUNK
    xcur = xc_ref[0]
    xtail = xp_ref[0] * (c > 0).astype(F32)
    row = lax.broadcasted_iota(I32, (8, 1), 0)
    cw = cw_ref[...]
    acc = cb_ref[...] + cw[SSM_CONV - 1:SSM_CONV, :] * xcur
    for s in range(1, SSM_CONV):
        rolled = pltpu.roll(xcur, s, 0)
        head = jnp.where(row >= s, rolled[0:8], pltpu.roll(xtail, s, 0))
        acc = acc + cw[SSM_CONV - 1 - s:SSM_CONV - s, :] * jnp.concatenate([head, rolled[8:]], axis=0)
    xc = _silu(acc)
    xs = xc[:, :SSM_WIDTH]
    bm = xc[:, SSM_WIDTH:SSM_WIDTH + 2 * SSM_STATE]
    cm = xc[:, SSM_WIDTH + 2 * SSM_STATE:]

    sm = sm_ref[0]
    lane = lax.broadcasted_iota(I32, (1, LANE), 1)
    head_lane = (lane >= DT_LANE0) & (lane < DT_LANE0 + N_SSM_HEADS)
    da = jnp.where(head_lane, sm * (-jnp.exp(alog_ref[...])), 0.0)
    ri = lax.broadcasted_iota(I32, (q, q), 0)
    ci = lax.broadcasted_iota(I32, (q, q), 1)
    causal = ri >= ci
    tri = jnp.where(causal, 1.0, 0.0).astype(BF16)
    d1, d2, d3 = _split3(da)
    acum = _dot(tri, d1) + _dot(tri, d2) + _dot(tri, d3)
    acum_t = acum.T
    last = acum[q - 1:q, :]
    eac = jnp.exp(acum)
    to_end = jnp.exp(last - acum)
    elast = jnp.exp(last)

    def spread(v):
        hi, lo = _split2(v)
        return _dot(hi, exp_ref[...]) + _dot(lo, exp_ref[...])

    xdt = xs * spread(sm)
    xdt_b = xdt.astype(BF16)
    xte_b = (xdt * spread(to_end)).astype(BF16)
    eac_w = spread(eac)
    elast_w = spread(jnp.broadcast_to(elast, (8, LANE)))[0:1]

    hpg = N_SSM_HEADS // 2
    gw = hpg * HEAD_DIM
    ys = []
    for g in range(2):
        bg = bm[:, g * SSM_STATE:(g + 1) * SSM_STATE]
        cg = cm[:, g * SSM_STATE:(g + 1) * SSM_STATE].astype(BF16)
        gmat = _dot_nt(cg, bg.astype(BF16))
        bgt = bg.T.astype(BF16)
        st = st_ref[g]
        ys.append(eac_w[:, g * gw:(g + 1) * gw] * _dot(cg, st.astype(BF16)))
        st_ref[g] = elast_w[:, g * gw:(g + 1) * gw] * st + _dot(bgt, xte_b[:, g * gw:(g + 1) * gw])
        for hh in range(hpg):
            h = g * hpg + hh
            ln = DT_LANE0 + h
            dec = jnp.exp(jnp.where(causal, acum[:, ln:ln + 1] - acum_t[ln:ln + 1, :], NEG))
            ys.append(_dot((gmat * dec).astype(BF16), xdt_b[:, h * HEAD_DIM:(h + 1) * HEAD_DIM]))
    intra = [jnp.concatenate(ys[g * (hpg + 1) + 1:(g + 1) * (hpg + 1)], axis=1) for g in range(2)]
    y = (jnp.concatenate(intra, axis=1) + jnp.concatenate([ys[0], ys[hpg + 1]], axis=1)
         + dsk_ref[...] * xs)
    yz = y * _silu(z_ref[0])
    half = SSM_WIDTH // 2
    outs = []
    for grp in range(2):
        v = yz[:, grp * half:(grp + 1) * half]
        outs.append(v * lax.rsqrt(jnp.mean(v * v, axis=-1, keepdims=True) + RMS_EPS)
                    * nw_ref[:, grp * half:(grp + 1) * half])
    o_ref[0] = jnp.concatenate(outs, axis=1).astype(BF16)


def ssd_mixer(xbc, z, sm, conv_w, conv_b, a_log, d_skip, ssm_norm_w, b, s):
    q = SSM_CHUNK
    cdim = xbc.shape[-1]
    alog = jnp.zeros((1, LANE), F32).at[0, DT_LANE0:DT_LANE0 + N_SSM_HEADS].set(a_log)
    dsk = jnp.repeat(d_skip, HEAD_DIM)[None, :]
    spread = np.zeros((LANE, SSM_WIDTH), np.float32)
    for h in range(N_SSM_HEADS):
        spread[DT_LANE0 + h, h * HEAD_DIM:(h + 1) * HEAD_DIM] = 1.0
    blk = lambda w: pl.BlockSpec((1, q, w), lambda bi, c: (bi, c, 0))
    full = lambda shp: pl.BlockSpec(shp, lambda bi, c: (0, 0))
    tail_rows = 8
    return pl.pallas_call(
        _ssd_kernel,
        grid=(b, s // q),
        in_specs=[
            blk(cdim),
            pl.BlockSpec((1, tail_rows, cdim), lambda bi, c: (bi, jnp.maximum(c * (q // tail_rows) - 1, 0), 0)),
            blk(SSM_WIDTH), blk(LANE),
            full((SSM_CONV, cdim)), full((1, cdim)), full((1, LANE)), full((1, SSM_WIDTH)), full((1, SSM_WIDTH)),
            full((LANE, SSM_WIDTH)),
        ],
        out_specs=blk(SSM_WIDTH),
        out_shape=jax.ShapeDtypeStruct((b, s, SSM_WIDTH), BF16),
        scratch_shapes=[pltpu.VMEM((2, SSM_STATE, SSM_WIDTH // 2), F32)],
        compiler_params=_cparams(("parallel", "arbitrary")),
        name="ssd_mixer",
    )(xbc.reshape(b, s, cdim), xbc.reshape(b, s, cdim), z.reshape(b, s, SSM_WIDTH), sm.reshape(b, s, LANE),
      conv_w, conv_b[None, :], alog, dsk, ssm_norm_w[None, :], jnp.asarray(spread, BF16))


def _outproj_kernel(att_ref, ssm_ref, x_ref, mod_ref, anw_ref, wa_ref, ws_ref, lnw_ref, lnb_ref,
                    x1_ref, u2_ref):
    att = att_ref[...]
    an = att * lax.rsqrt(jnp.mean(att * att, axis=-1, keepdims=True) + RMS_EPS) * anw_ref[...]
    h = _dot(an.astype(BF16), wa_ref[...]) + _dot(ssm_ref[...], ws_ref[...])
    gate1 = mod_ref[0, 2:3, :]
    x1 = _ln_rows(DN_ALPHA * x_ref[...] + (1.0 + gate1) * h) * lnw_ref[...] + lnb_ref[...]
    x1_ref[...] = x1
    shift2 = mod_ref[0, 3:4, :]
    scale2 = mod_ref[0, 4:5, :]
    u2_ref[...] = (_ln_rows(x1) * (1.0 + scale2) + shift2).astype(BF16)


def out_proj(att, ssm, x2d, mod, attn_norm_w, w_out, ln_w, ln_b, seq):
    t, d = x2d.shape
    tpb = seq // TM
    wo = w_out.astype(BF16)
    full = lambda shp: pl.BlockSpec(shp, lambda i: (0, 0))
    row = lambda n: pl.BlockSpec((TM, n), lambda i: (i, 0))
    return pl.pallas_call(
        _outproj_kernel,
        grid=(t // TM,),
        in_specs=[
            row(ATT_WIDTH), row(SSM_WIDTH), row(d),
            pl.BlockSpec((1, 6, d), lambda i: (i // tpb, 0, 0)),
            full((1, ATT_WIDTH)), full((ATT_WIDTH, d)), full((SSM_WIDTH, d)), full((1, d)), full((1, d)),
        ],
        out_specs=[row(d), row(d)],
        out_shape=[jax.ShapeDtypeStruct((t, d), F32), jax.ShapeDtypeStruct((t, d), BF16)],
        compiler_params=_cparams(("parallel",)),
        name="out_proj",
    )(att, ssm, x2d, mod, attn_norm_w[None, :], wo[:ATT_WIDTH], wo[ATT_WIDTH:], ln_w[None, :], ln_b[None, :])


def _chunks(n, c):
    return [(f0, min(c, n - f0)) for f0 in range(0, n, c)]


def _ffn_kernel(u_ref, x_ref, mod_ref, wg_ref, wu_ref, wd_ref, lnw_ref, lnb_ref, o_ref, acc_ref):
    u = u_ref[...]
    for k, (f0, fc) in enumerate(_chunks(wg_ref.shape[1], FCHUNK)):
        hid = (_silu(_dot(u, wg_ref[:, f0:f0 + fc])) * _dot(u, wu_ref[:, f0:f0 + fc])).astype(BF16)
        part = _dot(hid, wd_ref[f0:f0 + fc, :])
        if k == 0:
            acc_ref[...] = part
        else:
            acc_ref[...] += part
    gate2 = mod_ref[0, 5:6, :]
    o_ref[...] = _ln_rows(DN_ALPHA * x_ref[...] + (1.0 + gate2) * acc_ref[...]) * lnw_ref[...] + lnb_ref[...]


def dense_ffn(u2, x1, mod, wg, wu, wd, ln_w, ln_b, seq):
    t, d = x1.shape
    f = wg.shape[1]
    tpb = seq // TM
    full = lambda shp: pl.BlockSpec(shp, lambda i: (0, 0))
    row = lambda n: pl.BlockSpec((TM, n), lambda i: (i, 0))
    return pl.pallas_call(
        _ffn_kernel,
        grid=(t // TM,),
        in_specs=[
            row(d), row(d),
            pl.BlockSpec((1, 6, d), lambda i: (i // tpb, 0, 0)),
            full((d, f)), full((d, f)), full((f, d)), full((1, d)), full((1, d)),
        ],
        out_specs=row(d),
        out_shape=jax.ShapeDtypeStruct((t, d), F32),
        scratch_shapes=[pltpu.VMEM((TM, d), F32)],
        compiler_params=_cparams(("parallel",)),
        name="dense_ffn",
    )(u2, x1, mod, wg.astype(BF16), wu.astype(BF16), wd.astype(BF16), ln_w[None, :], ln_b[None, :])


MOE_WIN = MOE_TILE


def _router_kernel(u_ref, rw_ref, g_ref, mt_ref, cnt_ref):
    lt = _dot_nt(rw_ref[...], u_ref[...])
    e = lax.broadcasted_iota(I32, lt.shape, 0)
    top1 = jnp.max(lt, axis=0, keepdims=True)
    m1 = e == jnp.min(jnp.where(lt == top1, e, N_EXPERTS), axis=0, keepdims=True)
    l2 = jnp.where(m1, -3.0e38, lt)
    top2 = jnp.max(l2, axis=0, keepdims=True)
    m2 = e == jnp.min(jnp.where(l2 == top2, e, N_EXPERTS), axis=0, keepdims=True)
    e2 = jnp.exp(top2 - top1)
    den = 1.0 + e2
    gates = jnp.where(m1, 1.0 / den, 0.0) + jnp.where(m2, e2 / den, 0.0)
    member = jnp.where(m1 | m2, 1.0, 0.0)
    pad = jnp.zeros((LANE - 2 * N_EXPERTS, lt.shape[1]), F32)
    g_ref[...] = jnp.concatenate([gates, member, pad], axis=0).T
    mt_ref[...] = member
    cnt_ref[0] = jnp.broadcast_to(jnp.sum(member, axis=1, keepdims=True), (N_EXPERTS, LANE))


def moe_router(u2, router_w):
    t, d = u2.shape
    nt = t // MOE_TILE
    return pl.pallas_call(
        _router_kernel,
        grid=(nt,),
        in_specs=[pl.BlockSpec((MOE_TILE, d), lambda i: (i, 0)), pl.BlockSpec((N_EXPERTS, d), lambda i: (0, 0))],
        out_specs=[
            pl.BlockSpec((MOE_TILE, LANE), lambda i: (i, 0)),
            pl.BlockSpec((N_EXPERTS, MOE_TILE), lambda i: (0, i)),
            pl.BlockSpec((1, N_EXPERTS, LANE), lambda i: (i, 0, 0)),
        ],
        out_shape=[
            jax.ShapeDtypeStruct((t, LANE), F32),
            jax.ShapeDtypeStruct((N_EXPERTS, t), F32),
            jax.ShapeDtypeStruct((nt, N_EXPERTS, LANE), F32),
        ],
        compiler_params=_cparams(("parallel",)),
        name="moe_router",
    )(u2, router_w.T.astype(BF16))


def _dispatch_copies(base_ref, buf_ref, xs_ref, sem_ref, step, slot):
    return [
        pltpu.make_async_copy(
            buf_ref.at[slot, e],
            xs_ref.at[pl.ds(pl.multiple_of(base_ref[step * N_EXPERTS + e], MOE_ALIGN), MOE_WIN)],
            sem_ref.at[slot])
        for e in range(N_EXPERTS)
    ]


def _dispatch_kernel(base_ref, u_ref, mt_ref, xs_in_ref, xs_ref, buf_ref, sem_ref):
    del xs_in_ref
    i = pl.program_id(0)
    slot = i % 2
    mt = mt_ref[...]
    n = mt.shape[1]
    ri = lax.broadcasted_iota(I32, (n, n), 0)
    ci = lax.broadcasted_iota(I32, (n, n), 1)
    before = jnp.where(ri < ci, 1.0, 0.0).astype(BF16)
    rank = _dot(mt.astype(BF16), before).astype(I32)
    u = u_ref[...]
    wr = lax.broadcasted_iota(I32, (MOE_WIN, n), 0)
    for e in range(N_EXPERTS):
        pick = jnp.where((wr == rank[e:e + 1, :]) & (mt[e:e + 1, :] > 0.0), 1.0, 0.0).astype(BF16)
        buf_ref[slot, e] = _dot(pick, u).astype(BF16)

    @pl.when(i > 0)
    def _():
        for cp in _dispatch_copies(base_ref, buf_ref, xs_ref, sem_ref, i - 1, 1 - slot):
            cp.wait()

    for cp in _dispatch_copies(base_ref, buf_ref, xs_ref, sem_ref, i, slot):
        cp.start()

    @pl.when(i == pl.num_programs(0) - 1)
    def _():
        for cp in _dispatch_copies(base_ref, buf_ref, xs_ref, sem_ref, i, slot):
            cp.wait()


def moe_dispatch(u2, member_t, base, n_rows):
    t, d = u2.shape
    nt = t // MOE_TILE
    xs0 = jnp.zeros((n_rows, d), BF16)
    grid_spec = pltpu.PrefetchScalarGridSpec(
        num_scalar_prefetch=1,
        grid=(nt,),
        in_specs=[
            pl.BlockSpec((MOE_TILE, d), lambda i, base: (i, 0)),
            pl.BlockSpec((N_EXPERTS, MOE_TILE), lambda i, base: (0, i)),
            pl.BlockSpec(memory_space=pl.ANY),
        ],
        out_specs=pl.BlockSpec(memory_space=pl.ANY),
        scratch_shapes=[
            pltpu.VMEM((2, N_EXPERTS, MOE_WIN, d), BF16),
            pltpu.SemaphoreType.DMA((2,)),
        ],
    )
    return pl.pallas_call(
        _dispatch_kernel,
        grid_spec=grid_spec,
        out_shape=jax.ShapeDtypeStruct((n_rows, d), BF16),
        input_output_aliases={3: 0},
        compiler_params=_cparams(("arbitrary",)),
        name="moe_dispatch",
    )(base.reshape(-1), u2, member_t, xs0)


def _expert_kernel(be_ref, act_ref, x_ref, wg_ref, wu_ref, wd_ref, y_ref, acc_ref):
    i = pl.program_id(0)

    @pl.when(act_ref[i] > 0)
    def _():
        x = x_ref[...]
        for k, (f0, fc) in enumerate(_chunks(wg_ref.shape[2], FCHUNK)):
            hid = (_silu(_dot(x, wg_ref[0, :, f0:f0 + fc])) * _dot(x, wu_ref[0, :, f0:f0 + fc])).astype(BF16)
            part = _dot(hid, wd_ref[0, f0:f0 + fc, :])
            if k == 0:
                acc_ref[...] = part
            else:
                acc_ref[...] += part
        y_ref[...] = acc_ref[...].astype(BF16)

    @pl.when(act_ref[i] == 0)
    def _():
        y_ref[...] = jnp.zeros_like(y_ref)


def moe_experts(xs, blk_expert, blk_active, wg, wu, wd):
    n_rows, d = xs.shape
    f = wg.shape[2]
    resident = dict(pipeline_mode=pl.Buffered(1))
    grid_spec = pltpu.PrefetchScalarGridSpec(
        num_scalar_prefetch=2,
        grid=(n_rows // MOE_BLK,),
        in_specs=[
            pl.BlockSpec((MOE_BLK, d), lambda i, be, act: (i, 0)),
            pl.BlockSpec((1, d, f), lambda i, be, act: (be[i], 0, 0), **resident),
            pl.BlockSpec((1, d, f), lambda i, be, act: (be[i], 0, 0), **resident),
            pl.BlockSpec((1, f, d), lambda i, be, act: (be[i], 0, 0), **resident),
        ],
        out_specs=pl.BlockSpec((MOE_BLK, d), lambda i, be, act: (i, 0)),
        scratch_shapes=[pltpu.VMEM((MOE_BLK, d), F32)],
    )
    return pl.pallas_call(
        _expert_kernel,
        grid_spec=grid_spec,
        out_shape=jax.ShapeDtypeStruct((n_rows, d), BF16),
        compiler_params=_cparams(("arbitrary",)),
        name="moe_experts",
    )(blk_expert, blk_active, xs, wg.astype(BF16), wu.astype(BF16), wd.astype(BF16))


def _combine_copies(base_ref, y_ref, buf_ref, sem_ref, step, slot):
    return [
        pltpu.make_async_copy(
            y_ref.at[pl.ds(pl.multiple_of(base_ref[step * N_EXPERTS + e], MOE_ALIGN), MOE_WIN)],
            buf_ref.at[slot, e],
            sem_ref.at[slot])
        for e in range(N_EXPERTS)
    ]


def _combine_kernel(base_ref, g_ref, x_ref, mod_ref, lnw_ref, lnb_ref, y_ref, o_ref, buf_ref, sem_ref):
    i = pl.program_id(0)
    slot = i % 2

    @pl.when(i == 0)
    def _():
        for cp in _combine_copies(base_ref, y_ref, buf_ref, sem_ref, 0, 0):
            cp.start()

    @pl.when(i + 1 < pl.num_programs(0))
    def _():
        for cp in _combine_copies(base_ref, y_ref, buf_ref, sem_ref, i + 1, 1 - slot):
            cp.start()

    gm = g_ref[...]
    n = gm.shape[0]
    ri = lax.broadcasted_iota(I32, (n, n), 0)
    ci = lax.broadcasted_iota(I32, (n, n), 1)
    before = jnp.where(ci < ri, 1.0, 0.0).astype(BF16)
    lane = lax.broadcasted_iota(I32, (1, LANE), 1)
    member = jnp.where((lane >= N_EXPERTS) & (lane < 2 * N_EXPERTS), gm, 0.0)
    rank = _dot(before, member.astype(BF16)).astype(I32)
    wc = lax.broadcasted_iota(I32, (n, MOE_WIN), 1)

    for cp in _combine_copies(base_ref, y_ref, buf_ref, sem_ref, i, slot):
        cp.wait()

    f = jnp.zeros((n, x_ref.shape[1]), F32)
    for e in range(N_EXPERTS):
        le = N_EXPERTS + e
        pick = jnp.where((wc == rank[:, le:le + 1]) & (gm[:, le:le + 1] > 0.0), 1.0, 0.0).astype(BF16)
        f = f + gm[:, e:e + 1] * _dot(pick, buf_ref[slot, e])
    gate2 = mod_ref[0, 5:6, :]
    o_ref[...] = _ln_rows(DN_ALPHA * x_ref[...] + (1.0 + gate2) * f) * lnw_ref[...] + lnb_ref[...]


def moe_combine(y, gates, base, x1, mod, ln_w, ln_b, seq):
    t, d = x1.shape
    nt = t // MOE_TILE
    tpb = seq // MOE_TILE
    grid_spec = pltpu.PrefetchScalarGridSpec(
        num_scalar_prefetch=1,
        grid=(nt,),
        in_specs=[
            pl.BlockSpec((MOE_TILE, LANE), lambda i, base: (i, 0)),
            pl.BlockSpec((MOE_TILE, d), lambda i, base: (i, 0)),
            pl.BlockSpec((1, 6, d), lambda i, base: (i // tpb, 0, 0)),
            pl.BlockSpec((1, d), lambda i, base: (0, 0)),
            pl.BlockSpec((1, d), lambda i, base: (0, 0)),
            pl.BlockSpec(memory_space=pl.ANY),
        ],
        out_specs=pl.BlockSpec((MOE_TILE, d), lambda i, base: (i, 0)),
        scratch_shapes=[
            pltpu.VMEM((2, N_EXPERTS, MOE_WIN, d), BF16),
            pltpu.SemaphoreType.DMA((2,)),
        ],
    )
    return pl.pallas_call(
        _combine_kernel,
        grid_spec=grid_spec,
        out_shape=jax.ShapeDtypeStruct((t, d), F32),
        compiler_params=_cparams(("arbitrary",)),
        name="moe_combine",
    )(base.reshape(-1), gates, x1, mod, ln_w[None, :], ln_b[None, :], y)


def moe_ffn(u2, x1, mod, router_w, wg, wu, wd, ln_w, ln_b, seq):
    t, d = u2.shape
    nt = t // MOE_TILE
    gates, member_t, cnt = moe_router(u2, router_w)
    cnt = cnt[:, :, 0].astype(I32)
    seg = (cnt + MOE_ALIGN - 1) // MOE_ALIGN * MOE_ALIGN
    region = (jnp.sum(seg, axis=0) + MOE_WIN + MOE_BLK - 1) // MOE_BLK * MOE_BLK
    region_end = jnp.cumsum(region)
    base = (region_end - region)[None, :] + jnp.cumsum(seg, axis=0) - seg
    n_blk = -(-(2 * t + nt * N_EXPERTS * (MOE_ALIGN - 1) + N_EXPERTS * (MOE_WIN + MOE_BLK - 1)) // MOE_BLK)
    blk_start = jnp.arange(n_blk, dtype=I32) * MOE_BLK
    blk_expert = jnp.minimum(jnp.sum((blk_start[:, None] >= region_end[None, :]).astype(I32), axis=1), N_EXPERTS - 1)
    blk_active = (blk_start < region_end[-1]).astype(I32)
    xs = moe_dispatch(u2, member_t, base.astype(I32), n_blk * MOE_BLK)
    y = moe_experts(xs, blk_expert, blk_active, wg, wu, wd)
    return moe_combine(y, gates, base.astype(I32), x1, mod, ln_w, ln_b, seq)


def _mixer_front(x, mod, w_in, dt_bias, cmp_pos_k, cmp_w1_k, cmp_w2_k, cmp_pos_v, cmp_w1_v, cmp_w2_v):
    b, s, d = x.shape
    t = b * s
    g = N_KV_HEADS
    q, kv, z, xbc, sm = in_proj(x.reshape(t, d), mod, w_in, dt_bias, s)
    kv = kv.reshape(b, s, 6, 128)
    kvc = compress(jnp.stack([kv[:, :, 0], kv[:, :, 1]]),
                   jnp.stack([cmp_pos_k, cmp_pos_v]), jnp.stack([cmp_w1_k, cmp_w1_v]),
                   jnp.stack([cmp_w2_k, cmp_w2_v]))
    heads = lambda a: a.reshape(b, -1, g, HEAD_DIM).transpose(0, 2, 1, 3)
    gates = sm[:, :GATE_COLS].reshape(b, s, g, 3 * ATT_GQA).transpose(0, 2, 3, 1)
    att = nsa_attention(q.reshape(b, s, ATT_WIDTH), gates, heads(kvc[0]), heads(kvc[1]),
                        heads(kv[:, :, 2]), heads(kv[:, :, 3]), heads(kv[:, :, 4]), heads(kv[:, :, 5]))
    return att, z, xbc, sm


def kernel(x, c, ada_w, ada_b, w_in, cmp_pos_k, cmp_w1_k, cmp_w2_k, cmp_pos_v, cmp_w1_v, cmp_w2_v, attn_norm_w, conv_w, conv_b, dt_bias, a_log, d_skip, ssm_norm_w, w_out, ln1_w, ln1_b, ln2_w, ln2_b, ffn_w_gate, ffn_w_up, ffn_w_down, router_w, exp_w_gate, exp_w_up, exp_w_down):
    b, s, d = x.shape
    t = b * s
    mod = ada_mod(c, ada_w, ada_b).reshape(DEPTH, b, 6, d)
    xc = x.reshape(t, d)
    for l in range(DEPTH):
        att, z, xbc, sm = _mixer_front(xc.reshape(b, s, d), mod[l], w_in[l], dt_bias[l],
                                       cmp_pos_k[l], cmp_w1_k[l], cmp_w2_k[l],
                                       cmp_pos_v[l], cmp_w1_v[l], cmp_w2_v[l])
        ssm = ssd_mixer(xbc, z, sm, conv_w[l], conv_b[l], a_log[l], d_skip[l], ssm_norm_w[l], b, s)
        x1, u2 = out_proj(att.reshape(t, ATT_WIDTH), ssm.reshape(t, SSM_WIDTH), xc, mod[l], attn_norm_w[l],
                          w_out[l], ln1_w[l], ln1_b[l], s)
        i = l // 2
        if l % 2 == 0:
            xc = dense_ffn(u2, x1, mod[l], ffn_w_gate[i], ffn_w_up[i], ffn_w_down[i], ln2_w[l], ln2_b[l], s)
        else:
            xc = moe_ffn(u2, x1, mod[l], router_w[i], exp_w_gate[i], exp_w_up[i], exp_w_down[i],
                         ln2_w[l], ln2_b[l], s)
    return xc.reshape(b, s, d)
```

```python
import functools
import math

import numpy as np
import jax
import jax.numpy as jnp
from jax import lax
from jax.experimental import pallas as pl
from jax.experimental.pallas import tpu as pltpu

F32 = jnp.float32
BF16 = jnp.bfloat16
I32 = jnp.int32

D_MODEL = 1024
HEAD_DIM = 64
ATT_WIDTH = 512
N_KV_HEADS = 2
ATT_GQA = 4
CMP_BLOCK = 32
CMP_STRIDE = 16
CMP_HIDDEN = 256
SEL_BLOCK = 64
SEL_TOPN = 16
SEL_FORCE_BONUS = 1e4
WINDOW = 512
SSM_WIDTH = 512
N_SSM_HEADS = 8
SSM_STATE = 128
SSM_CONV = 4
SSM_CHUNK = 128
FFN_DENSE = 2816
N_EXPERTS = 8
FFN_EXPERT = 3584
DEPTH = 2
DN_ALPHA = (2 * DEPTH) ** 0.25
LN_EPS = 1e-5
RMS_EPS = 1e-6
NEG = -1e30

LANE = 128
VMEM_LIMIT = 56 * 1024 * 1024

TM = 512
TQ = 256
TK = 512
MOE_TILE = 512
MOE_ALIGN = 16
MOE_BLK = 1024
FCHUNK = 512


def _cparams(sem):
    return pltpu.CompilerParams(dimension_semantics=sem, vmem_limit_bytes=VMEM_LIMIT)


def _dot(a, b):
    return jnp.dot(a, b, preferred_element_type=F32)


def _dot_nt(a, b):
    return lax.dot_general(a, b, (((1,), (1,)), ((), ())), preferred_element_type=F32)


def _split2(x):
    hi = x.astype(BF16)
    lo = (x - hi.astype(F32)).astype(BF16)
    return hi, lo


def _split3(x):
    hi = x.astype(BF16)
    r = x - hi.astype(F32)
    mid = r.astype(BF16)
    lo = (r - mid.astype(F32)).astype(BF16)
    return hi, mid, lo


def _sigmoid(x):
    return 1.0 / (1.0 + jnp.exp(-x))


def _silu(x):
    return x * _sigmoid(x)


def _ln_rows(x):
    mu = jnp.mean(x, axis=-1, keepdims=True)
    xc = x - mu
    var = jnp.mean(xc * xc, axis=-1, keepdims=True)
    return xc * lax.rsqrt(var + LN_EPS)


def _ada_kernel(c_ref, w_ref, b_ref, o_ref):
    cond = _silu(c_ref[...])
    ch, cl = _split2(cond)
    wh, wl = _split2(w_ref[0])
    o_ref[0] = _dot(ch, wh) + _dot(cl, wh) + _dot(ch, wl) + b_ref[0]


def ada_mod(c, ada_w, ada_b):
    nl, d, n = ada_w.shape
    b = c.shape[0]
    tn = 1536
    return pl.pallas_call(
        _ada_kernel,
        grid=(nl, n // tn),
        in_specs=[
            pl.BlockSpec((b, d), lambda l, j: (0, 0)),
            pl.BlockSpec((1, d, tn), lambda l, j: (l, 0, j)),
            pl.BlockSpec((1, 1, tn), lambda l, j: (l, 0, j)),
        ],
        out_specs=pl.BlockSpec((1, b, tn), lambda l, j: (l, 0, j)),
        out_shape=jax.ShapeDtypeStruct((nl, b, n), F32),
        compiler_params=_cparams(("parallel", "parallel")),
        name="ada_mod",
    )(c, ada_w, ada_b.reshape(nl, 1, n))


KV_COLS = 6 * 128
GATE_COLS = 24
DT_LANE0 = 32


def _inproj_kernel(x_ref, mod_ref, wq_ref, wkv_ref, wz_ref, wx_ref, wsh_ref, wsl_ref, dtb_ref,
                   q_ref, kv_ref, z_ref, xbc_ref, sm_ref):
    xn = _ln_rows(x_ref[...])
    shift = mod_ref[0, 0:1, :]
    scale = mod_ref[0, 1:2, :]
    u = xn * (1.0 + scale) + shift
    ub = u.astype(BF16)
    ul = (u - ub.astype(F32)).astype(BF16)
    q_ref[...] = (_dot(ub, wq_ref[...]) * (LOG2E * HEAD_DIM ** -0.5)).astype(BF16)
    kv_ref[...] = _dot(ub, wkv_ref[...]).astype(BF16)
    z_ref[...] = _dot(ub, wz_ref[...])
    xbc_ref[...] = _dot(ub, wx_ref[...])
    sm = _dot(ub, wsh_ref[...]) + _dot(ul, wsh_ref[...]) + _dot(ub, wsl_ref[...])
    lane = lax.broadcasted_iota(I32, sm.shape, 1)
    xdt = sm + dtb_ref[...]
    dt = jnp.maximum(xdt, 0.0) + jnp.log(1.0 + jnp.exp(-jnp.abs(xdt)))
    sm_ref[...] = jnp.where(lane < GATE_COLS, _sigmoid(sm), dt)


def in_proj(x2d, mod, w_in, dt_bias, seq):
    t, d = x2d.shape
    wq = w_in[:, :512].astype(BF16)
    wkv = w_in[:, 512:512 + KV_COLS].astype(BF16)
    o = 512 + KV_COLS
    wg = w_in[:, o:o + GATE_COLS]
    wz = w_in[:, o + GATE_COLS:o + GATE_COLS + 512].astype(BF16)
    o2 = o + GATE_COLS + 512
    wx = w_in[:, o2:o2 + 1024].astype(BF16)
    wdt = w_in[:, o2 + 1024:]
    wsm = jnp.zeros((d, LANE), F32).at[:, :GATE_COLS].set(wg).at[:, DT_LANE0:DT_LANE0 + N_SSM_HEADS].set(wdt)
    wsh, wsl = _split2(wsm)
    dtb = jnp.zeros((1, LANE), F32).at[0, DT_LANE0:DT_LANE0 + N_SSM_HEADS].set(dt_bias)
    tpb = seq // TM
    full = lambda shp: pl.BlockSpec(shp, lambda i: (0, 0))
    row = lambda n: pl.BlockSpec((TM, n), lambda i: (i, 0))
    return pl.pallas_call(
        _inproj_kernel,
        grid=(t // TM,),
        in_specs=[
            row(d),
            pl.BlockSpec((1, 6, d), lambda i: (i // tpb, 0, 0)),
            full((d, 512)), full((d, KV_COLS)), full((d, 512)), full((d, 1024)),
            full((d, LANE)), full((d, LANE)), full((1, LANE)),
        ],
        out_specs=[row(512), row(KV_COLS), row(512), row(1024), row(LANE)],
        out_shape=[
            jax.ShapeDtypeStruct((t, 512), BF16),
            jax.ShapeDtypeStruct((t, KV_COLS), BF16),
            jax.ShapeDtypeStruct((t, 512), F32),
            jax.ShapeDtypeStruct((t, 1024), F32),
            jax.ShapeDtypeStruct((t, LANE), F32),
        ],
        compiler_params=_cparams(("parallel",)),
        name="in_proj",
    )(x2d, mod, wq, wkv, wz, wx, wsh, wsl, dtb)


def _gelu_tanh(x):
    return 0.5 * x * (1.0 + jnp.tanh(math.sqrt(2.0 / math.pi) * (x + 0.044715 * (x * x * x))))


def _cmp_kernel(x_ref, w1e_ref, w1_ref, pos_ref, w2e_ref, o_ref):
    x = x_ref[0, 0]
    nc = x.shape[0]
    ab = _dot(x, w1e_ref[0])
    half = N_KV_HEADS * CMP_HIDDEN
    first = ab[:, :half]
    second = pltpu.roll(ab[:, half:], nc - 1, 0)
    ph, plo = _split2(pos_ref[0])
    pb = (_dot(ph, w1_ref[0]) + _dot(plo, w1_ref[0]))[0:1, :]
    pb = jnp.concatenate([pb] * N_KV_HEADS, axis=1)
    h = _gelu_tanh(first + second + pb)
    o_ref[0, 0] = _dot(h.astype(BF16), w2e_ref[0]).astype(BF16)


def compress(kv_raw, pos, w1, w2):
    _, b, s, _ = kv_raw.shape
    nc = s // CMP_STRIDE
    x = kv_raw.reshape(2, b, nc, CMP_STRIDE * 128)
    g = N_KV_HEADS
    w1r = w1.reshape(2, 2, CMP_STRIDE, HEAD_DIM, CMP_HIDDEN)
    eye = jnp.eye(g, dtype=F32)
    w1e = jnp.einsum('kjtdh,ga->ktgdjah', w1r, eye).reshape(2, CMP_STRIDE * g * HEAD_DIM, 2 * g * CMP_HIDDEN)
    w2e = jnp.einsum('khd,ga->kghad', w2, eye).reshape(2, g * CMP_HIDDEN, g * HEAD_DIM)
    posf = jnp.zeros((2, 8, CMP_BLOCK * HEAD_DIM), F32).at[:, 0, :].set(pos.reshape(2, -1))
    out = pl.pallas_call(
        _cmp_kernel,
        grid=(2, b),
        in_specs=[
            pl.BlockSpec((1, 1, nc, CMP_STRIDE * 128), lambda k, i: (k, i, 0, 0)),
            pl.BlockSpec((1,) + w1e.shape[1:], lambda k, i: (k, 0, 0)),
            pl.BlockSpec((1,) + w1.shape[1:], lambda k, i: (k, 0, 0)),
            pl.BlockSpec((1,) + posf.shape[1:], lambda k, i: (k, 0, 0)),
            pl.BlockSpec((1,) + w2e.shape[1:], lambda k, i: (k, 0, 0)),
        ],
        out_specs=pl.BlockSpec((1, 1, nc, 128), lambda k, i: (k, i, 0, 0)),
        out_shape=jax.ShapeDtypeStruct((2, b, nc, 128), BF16),
        compiler_params=_cparams(("parallel", "parallel")),
        name="compress",
    )(x, w1e.astype(BF16), w1.astype(BF16), posf, w2e.astype(BF16))
    return out


R_ROWS = ATT_GQA * TQ
WIN_KEYS = WINDOW + TQ
KAUG = LANE + HEAD_DIM
VROWS = HEAD_DIM + 16
LOG2E = math.log2(math.e)
SEL_PAD = TK - TQ
SEL_SPLIT = 2
PRE_SPLIT = 1


def _exp_cols(s):
    return jnp.exp2(s - jnp.max(s, axis=0, keepdims=True))


def _sel_scores(k0, width, qaug_ref, kaug_ref):
    return _dot(kaug_ref[0, 0, pl.ds(k0, width), :], qaug_ref[...])


def _sel_update(st, k0, m_old, vt_ref, acc_ref):
    m_new = jnp.maximum(m_old, jnp.max(st, axis=0, keepdims=True))
    pt = jnp.exp2(st - m_new).astype(BF16)
    c0 = k0 // LANE
    vt = jnp.concatenate([vt_ref[0, 0, c0 + c] for c in range(st.shape[0] // LANE)], axis=1)
    acc_ref[...] = jnp.exp2(m_old - m_new) * acc_ref[...] + _dot(vt, pt)
    return m_new


def _sel_update_cols(st, k0, m_old, vt_ref, acc_ref, ls):
    m_new = jnp.maximum(m_old, jnp.max(st, axis=0, keepdims=True))
    pt = jnp.exp2(st - m_new).astype(BF16)
    c0 = k0 // LANE
    vt = jnp.concatenate([vt_ref[0, 0, c0 + c] for c in range(st.shape[0] // LANE)], axis=1)
    acc_ref[:, ls] = jnp.exp2(m_old - m_new) * acc_ref[:, ls] + _dot(vt, pt)
    return m_new


def _attn_kernel(q_ref, gate_ref, kc_ref, vct_ref, ovt_ref, cband_ref, wband_ref, kaug_ref, vt_ref, kw_ref, vwt_ref,
                 o_ref, qaug_ref, acc_ref, s0_ref, s1_ref, part_ref):
    i = pl.program_id(2)
    t0 = i * TQ
    qt = q_ref[0].astype(F32).T.astype(BF16)
    qst = jnp.concatenate([qt[r * HEAD_DIM:(r + 1) * HEAD_DIM, :] for r in range(ATT_GQA)], axis=1)
    t_lane = t0 + lax.broadcasted_iota(I32, (1, R_ROWS), 1) % TQ

    kc = kc_ref[0, 0]
    nc = kc.shape[0]
    hpg = ATT_GQA // PRE_SPLIT
    cband = jnp.concatenate([cband_ref[pl.ds(pl.multiple_of(nc - t0 // CMP_STRIDE, 8), nc), :]] * hpg, axis=1)
    ws = pl.multiple_of(jnp.maximum(t0 - WINDOW, 0), TQ)
    wband = jnp.concatenate(
        [wband_ref[pl.ds(pl.multiple_of(WINDOW - (t0 - ws), TQ), WIN_KEYS), :]] * hpg, axis=1)
    cw = ws // LANE
    gt = gate_ref[0, 0]
    psum = None
    for grp in range(PRE_SPLIT):
        ls = slice(grp * hpg * TQ, (grp + 1) * hpg * TQ)
        qg = qst[:, ls]
        e = _exp_cols(_dot(kc, qg) + cband)
        p = e * jnp.where(t_lane[:, ls] >= CMP_BLOCK - 1, 1.0 / jnp.sum(e, axis=0, keepdims=True), 0.0)
        o_cmp = _dot(vct_ref[0, 0], p.astype(BF16))
        ew = _exp_cols(_dot(kw_ref[0, 0, pl.ds(ws, WIN_KEYS), :], qg) + wband).astype(BF16)
        o_win = _dot(vwt_ref[0, 0, cw], ew[0:LANE])
        for c in range(1, WIN_KEYS // LANE):
            o_win = o_win + _dot(vwt_ref[0, 0, cw + c], ew[c * LANE:(c + 1) * LANE])
        o_win = o_win[:HEAD_DIM] * (1.0 / o_win[HEAD_DIM:HEAD_DIM + 1])
        for hh in range(hpg):
            r = grp * hpg + hh
            sl = slice(hh * TQ, (hh + 1) * TQ)
            part_ref[r * HEAD_DIM:(r + 1) * HEAD_DIM, :] = (gt[3 * r:3 * r + 1] * o_cmp[:, sl]
                                                            + gt[3 * r + 2:3 * r + 3] * o_win[:, sl])
            psum = p[:, sl] if psum is None else psum + p[:, sl]

    ph, plo = _split2(psum)
    ovt = ovt_ref[...]
    imp = _dot(ovt, ph) + _dot(ovt, plo)
    j = lax.broadcasted_iota(I32, (LANE, TQ), 0)
    jt = (t0 + lax.broadcasted_iota(I32, (LANE, TQ), 1)) // SEL_BLOCK
    forced = (j == 0) | (j == jt) | (j == jt - 1)
    valid = j <= jt
    score = jnp.where(forced, NEG, jnp.where(valid, imp, NEG))

    def pick(_, carry):
        sc, sel = carry
        mx = jnp.max(sc, axis=0, keepdims=True)
        idx = jnp.min(jnp.where(sc == mx, j, LANE), axis=0, keepdims=True)
        hit = j == idx
        return jnp.where(hit, -3.0e38, sc), jnp.where(hit, 1.0, sel)

    _, sel = lax.fori_loop(0, SEL_TOPN - 3, pick, (score, jnp.where(forced, 1.0, 0.0)), unroll=True)
    bias_t = jnp.where(valid, jnp.where(sel > 0.0, 0.0, NEG), NEG).astype(BF16)
    for r in range(ATT_GQA):
        qaug_ref[0:LANE, r * TQ:(r + 1) * TQ] = bias_t
    qaug_ref[LANE:KAUG, :] = qst

    scores = functools.partial(_sel_scores, qaug_ref=qaug_ref, kaug_ref=kaug_ref)
    update = functools.partial(_sel_update, vt_ref=vt_ref, acc_ref=acc_ref)
    acc_ref[...] = jnp.zeros(acc_ref.shape, F32)
    kd = pl.multiple_of(t0 + SEL_PAD, TQ)
    kpos = t0 + lax.broadcasted_iota(I32, (TQ, 1), 0)
    m = update(jnp.where(kpos <= t_lane, scores(kd, TQ), NEG), kd, jnp.full((1, R_ROWS), NEG, F32))
    n_t = (t0 + TK - 1) // TK
    kstart = lambda jj: pl.multiple_of(t0 + SEL_PAD - (jj + 1) * TK, TQ)

    @pl.when(n_t > 0)
    def _():
        s0_ref[...] = scores(kstart(0), TK)

    def pair(pp, mm):
        ka, kb, kc2 = kstart(2 * pp), kstart(jnp.minimum(2 * pp + 1, n_t - 1)), kstart(jnp.minimum(2 * pp + 2, n_t - 1))
        kb_u = kstart(2 * pp + 1)
        outs = []
        for hl in range(SEL_SPLIT):
            ls = slice(hl * (R_ROWS // SEL_SPLIT), (hl + 1) * (R_ROWS // SEL_SPLIT))
            s1_ref[:, ls] = _dot(kaug_ref[0, 0, pl.ds(kb, TK), :], qaug_ref[:, ls])
            mh = _sel_update_cols(s0_ref[:, ls], ka, mm[:, ls], vt_ref, acc_ref, ls)
            s0_ref[:, ls] = _dot(kaug_ref[0, 0, pl.ds(kc2, TK), :], qaug_ref[:, ls])
            outs.append(_sel_update_cols(s1_ref[:, ls], kb_u, mh, vt_ref, acc_ref, ls))
        return jnp.concatenate(outs, axis=1)

    m = lax.fori_loop(0, n_t // 2, pair, m)

    @pl.when(n_t % 2 == 1)
    def _():
        update(s0_ref[...], kstart(n_t - 1), m)

    acc = acc_ref[...]
    o_sel = acc[:HEAD_DIM] * (1.0 / acc[HEAD_DIM:HEAD_DIM + 1])

    outs = []
    for r in range(ATT_GQA):
        sl = slice(r * TQ, (r + 1) * TQ)
        outs.append(part_ref[r * HEAD_DIM:(r + 1) * HEAD_DIM, :] + gt[3 * r + 1:3 * r + 2] * o_sel[:, sl])
    o_ref[0] = jnp.concatenate(outs, axis=0).T


def nsa_attention(q, gates, kc, vc, ks, vs, kw, vw):
    b, s, _ = q.shape
    g = N_KV_HEADS
    nc = kc.shape[2]
    n_sel = s // SEL_BLOCK
    n_chunk = s // LANE
    assert SEL_TOPN <= n_sel <= LANE and s % TK == 0 and s >= WIN_KEYS
    chunks_t = lambda a: a.reshape(b, g, -1, LANE, a.shape[-1]).swapaxes(3, 4)
    front = lambda a, fill: jnp.concatenate(
        [jnp.broadcast_to(fill, (b, g, SEL_PAD, a.shape[-1])).astype(a.dtype), a], axis=2)
    cs = np.arange(nc)[None, :] * CMP_STRIDE
    js = np.arange(LANE)[:, None] * SEL_BLOCK
    ovt = ((cs < js + SEL_BLOCK) & (cs + CMP_BLOCK > js) & (np.arange(nc)[None, :] < nc - 1)
           & (np.arange(LANE)[:, None] < n_sel))
    ovt = jnp.asarray(ovt, BF16)
    onehot = jnp.asarray(np.arange(s)[:, None] // SEL_BLOCK == np.arange(LANE)[None, :], BF16)
    kaug = jnp.concatenate([jnp.broadcast_to(onehot, (b, g, s, LANE)), ks], axis=-1)
    pad_row = jnp.asarray(np.concatenate([np.ones(LANE), np.zeros(HEAD_DIM)]), BF16)
    kaug = front(kaug, pad_row)
    ones = jnp.broadcast_to(jnp.zeros((s, VROWS - HEAD_DIM), BF16).at[:, 0].set(1), (b, g, s, VROWS - HEAD_DIM))
    vt = chunks_t(front(jnp.concatenate([vs, ones], axis=-1), jnp.zeros((), BF16)))
    vwt = chunks_t(jnp.concatenate([vw, ones], axis=-1))
    tok = np.arange(TQ)[None, :]
    xc = np.arange(2 * nc)[:, None]
    cband = jnp.asarray(np.where(CMP_STRIDE * (xc - nc) + CMP_BLOCK - 1 <= tok, 0.0, NEG), F32)
    xw = np.arange(WINDOW + WIN_KEYS)[:, None]
    wband = jnp.asarray(np.where((xw <= WINDOW + tok) & (xw > tok), 0.0, NEG), F32)
    const = lambda a: pl.BlockSpec(a.shape, lambda bi, gi, i: (0, 0))
    per_bg = lambda *shp: pl.BlockSpec((1, 1) + shp, lambda bi, gi, i: (bi, gi) + (0,) * len(shp))
    return pl.pallas_call(
        _attn_kernel,
        grid=(b, g, s // TQ),
        in_specs=[
            pl.BlockSpec((1, TQ, ATT_GQA * HEAD_DIM), lambda bi, gi, i: (bi, i, gi)),
            pl.BlockSpec((1, 1, 3 * ATT_GQA, TQ), lambda bi, gi, i: (bi, gi, 0, i)),
            per_bg(nc, HEAD_DIM), per_bg(HEAD_DIM, nc),
            const(ovt), const(cband), const(wband),
            per_bg(s + SEL_PAD, KAUG), per_bg(n_chunk + SEL_PAD // LANE, VROWS, LANE),
            per_bg(s, HEAD_DIM), per_bg(n_chunk, VROWS, LANE),
        ],
        out_specs=pl.BlockSpec((1, TQ, ATT_GQA * HEAD_DIM), lambda bi, gi, i: (bi, i, gi)),
        out_shape=jax.ShapeDtypeStruct((b, s, ATT_WIDTH), F32),
        scratch_shapes=[
            pltpu.VMEM((KAUG, R_ROWS), BF16),
            pltpu.VMEM((VROWS, R_ROWS), F32),
            pltpu.VMEM((TK, R_ROWS), F32),
            pltpu.VMEM((TK, R_ROWS), F32),
            pltpu.VMEM((ATT_GQA * HEAD_DIM, TQ), F32),
        ],
        compiler_params=_cparams(("parallel", "parallel", "arbitrary")),
        name="nsa_attention",
    )(q, gates, kc, vc.swapaxes(2, 3), ovt, cband, wband, kaug, vt, kw, vwt)


def _ssd_kernel(xc_ref, xp_ref, z_ref, sm_ref, cw_ref, cb_ref, alog_ref, dsk_ref, nw_ref, exp_ref, o_ref, st_ref):
    c = pl.program_id(1)

    @pl.when(c == 0)
    def _():
        st_ref[...] = jnp.zeros_like(st_ref)

    q = SSM_CHUNK
    xcur = xc_ref[0]
    xtail = xp_ref[0] * (c > 0).astype(F32)
    row = lax.broadcasted_iota(I32, (8, 1), 0)
    cw = cw_ref[...]
    acc = cb_ref[...] + cw[SSM_CONV - 1:SSM_CONV, :] * xcur
    for s in range(1, SSM_CONV):
        rolled = pltpu.roll(xcur, s, 0)
        head = jnp.where(row >= s, rolled[0:8], pltpu.roll(xtail, s, 0))
        acc = acc + cw[SSM_CONV - 1 - s:SSM_CONV - s, :] * jnp.concatenate([head, rolled[8:]], axis=0)
    xc = _silu(acc)
    xs = xc[:, :SSM_WIDTH]
    bm = xc[:, SSM_WIDTH:SSM_WIDTH + 2 * SSM_STATE]
    cm = xc[:, SSM_WIDTH + 2 * SSM_STATE:]

    sm = sm_ref[0]
    lane = lax.broadcasted_iota(I32, (1, LANE), 1)
    head_lane = (lane >= DT_LANE0) & (lane < DT_LANE0 + N_SSM_HEADS)
    da = jnp.where(head_lane, sm * (-jnp.exp(alog_ref[...])), 0.0)
    ri = lax.broadcasted_iota(I32, (q, q), 0)
    ci = lax.broadcasted_iota(I32, (q, q), 1)
    causal = ri >= ci
    tri = jnp.where(causal, 1.0, 0.0).astype(BF16)
    d1, d2, d3 = _split3(da)
    acum = _dot(tri, d1) + _dot(tri, d2) + _dot(tri, d3)
    acum_t = acum.T
    last = acum[q - 1:q, :]
    eac = jnp.exp(acum)
    to_end = jnp.exp(last - acum)
    elast = jnp.exp(last)

    def spread(v):
        hi, lo = _split2(v)
        return _dot(hi, exp_ref[...]) + _dot(lo, exp_ref[...])

    xdt = xs * spread(sm)
    xdt_b = xdt.astype(BF16)
    xte_b = (xdt * spread(to_end)).astype(BF16)
    eac_w = spread(eac)
    elast_w = spread(jnp.broadcast_to(elast, (8, LANE)))[0:1]

    hpg = N_SSM_HEADS // 2
    gw = hpg * HEAD_DIM
    ys = []
    for g in range(2):
        bg = bm[:, g * SSM_STATE:(g + 1) * SSM_STATE]
        cg = cm[:, g * SSM_STATE:(g + 1) * SSM_STATE].astype(BF16)
        gmat = _dot_nt(cg, bg.astype(BF16))
        bgt = bg.T.astype(BF16)
        st = st_ref[g]
        ys.append(eac_w[:, g * gw:(g + 1) * gw] * _dot(cg, st.astype(BF16)))
        st_ref[g] = elast_w[:, g * gw:(g + 1) * gw] * st + _dot(bgt, xte_b[:, g * gw:(g + 1) * gw])
        for hh in range(hpg):
            h = g * hpg + hh
            ln = DT_LANE0 + h
            dec = jnp.exp(jnp.where(causal, acum[:, ln:ln + 1] - acum_t[ln:ln + 1, :], NEG))
            ys.append(_dot((gmat * dec).astype(BF16), xdt_b[:, h * HEAD_DIM:(h + 1) * HEAD_DIM]))
    intra = [jnp.concatenate(ys[g * (hpg + 1) + 1:(g + 1) * (hpg + 1)], axis=1) for g in range(2)]
    y = (jnp.concatenate(intra, axis=1) + jnp.concatenate([ys[0], ys[hpg + 1]], axis=1)
         + dsk_ref[...] * xs)
    yz = y * _silu(z_ref[0])
    half = SSM_WIDTH // 2
    outs = []
    for grp in range(2):
        v = yz[:, grp * half:(grp + 1) * half]
        outs.append(v * lax.rsqrt(jnp.mean(v * v, axis=-1, keepdims=True) + RMS_EPS)
                    * nw_ref[:, grp * half:(grp + 1) * half])
    o_ref[0] = jnp.concatenate(outs, axis=1).astype(BF16)


def ssd_mixer(xbc, z, sm, conv_w, conv_b, a_log, d_skip, ssm_norm_w, b, s):
    q = SSM_CHUNK
    cdim = xbc.shape[-1]
    alog = jnp.zeros((1, LANE), F32).at[0, DT_LANE0:DT_LANE0 + N_SSM_HEADS].set(a_log)
    dsk = jnp.repeat(d_skip, HEAD_DIM)[None, :]
    spread = np.zeros((LANE, SSM_WIDTH), np.float32)
    for h in range(N_SSM_HEADS):
        spread[DT_LANE0 + h, h * HEAD_DIM:(h + 1) * HEAD_DIM] = 1.0
    blk = lambda w: pl.BlockSpec((1, q, w), lambda bi, c: (bi, c, 0))
    full = lambda shp: pl.BlockSpec(shp, lambda bi, c: (0, 0))
    tail_rows = 8
    return pl.pallas_call(
        _ssd_kernel,
        grid=(b, s // q),
        in_specs=[
            blk(cdim),
            pl.BlockSpec((1, tail_rows, cdim), lambda bi, c: (bi, jnp.maximum(c * (q // tail_rows) - 1, 0), 0)),
            blk(SSM_WIDTH), blk(LANE),
            full((SSM_CONV, cdim)), full((1, cdim)), full((1, LANE)), full((1, SSM_WIDTH)), full((1, SSM_WIDTH)),
            full((LANE, SSM_WIDTH)),
        ],
        out_specs=blk(SSM_WIDTH),
        out_shape=jax.ShapeDtypeStruct((b, s, SSM_WIDTH), BF16),
        scratch_shapes=[pltpu.VMEM((2, SSM_STATE, SSM_WIDTH // 2), F32)],
        compiler_params=_cparams(("parallel", "arbitrary")),
        name="ssd_mixer",
    )(xbc.reshape(b, s, cdim), xbc.reshape(b, s, cdim), z.reshape(b, s, SSM_WIDTH), sm.reshape(b, s, LANE),
      conv_w, conv_b[None, :], alog, dsk, ssm_norm_w[None, :], jnp.asarray(spread, BF16))


def _outproj_kernel(att_ref, ssm_ref, x_ref, mod_ref, anw_ref, wa_ref, ws_ref, lnw_ref, lnb_ref,
                    x1_ref, u2_ref):
    att = att_ref[...]
    an = att * lax.rsqrt(jnp.mean(att * att, axis=-1, keepdims=True) + RMS_EPS) * anw_ref[...]
    h = _dot(an.astype(BF16), wa_ref[...]) + _dot(ssm_ref[...], ws_ref[...])
    gate1 = mod_ref[0, 2:3, :]
    x1 = _ln_rows(DN_ALPHA * x_ref[...] + (1.0 + gate1) * h) * lnw_ref[...] + lnb_ref[...]
    x1_ref[...] = x1
    shift2 = mod_ref[0, 3:4, :]
    scale2 = mod_ref[0, 4:5, :]
    u2_ref[...] = (_ln_rows(x1) * (1.0 + scale2) + shift2).astype(BF16)


def out_proj(att, ssm, x2d, mod, attn_norm_w, w_out, ln_w, ln_b, seq):
    t, d = x2d.shape
    tpb = seq // TM
    wo = w_out.astype(BF16)
    full = lambda shp: pl.BlockSpec(shp, lambda i: (0, 0))
    row = lambda n: pl.BlockSpec((TM, n), lambda i: (i, 0))
    return pl.pallas_call(
        _outproj_kernel,
        grid=(t // TM,),
        in_specs=[
            row(ATT_WIDTH), row(SSM_WIDTH), row(d),
            pl.BlockSpec((1, 6, d), lambda i: (i // tpb, 0, 0)),
            full((1, ATT_WIDTH)), full((ATT_WIDTH, d)), full((SSM_WIDTH, d)), full((1, d)), full((1, d)),
        ],
        out_specs=[row(d), row(d)],
        out_shape=[jax.ShapeDtypeStruct((t, d), F32), jax.ShapeDtypeStruct((t, d), BF16)],
        compiler_params=_cparams(("parallel",)),
        name="out_proj",
    )(att, ssm, x2d, mod, attn_norm_w[None, :], wo[:ATT_WIDTH], wo[ATT_WIDTH:], ln_w[None, :], ln_b[None, :])


def _chunks(n, c):
    return [(f0, min(c, n - f0)) for f0 in range(0, n, c)]


def _ffn_kernel(u_ref, x_ref, mod_ref, wg_ref, wu_ref, wd_ref, lnw_ref, lnb_ref, o_ref, acc_ref):
    u = u_ref[...]
    for k, (f0, fc) in enumerate(_chunks(wg_ref.shape[1], FCHUNK)):
        hid = (_silu(_dot(u, wg_ref[:, f0:f0 + fc])) * _dot(u, wu_ref[:, f0:f0 + fc])).astype(BF16)
        part = _dot(hid, wd_ref[f0:f0 + fc, :])
        if k == 0:
            acc_ref[...] = part
        else:
            acc_ref[...] += part
    gate2 = mod_ref[0, 5:6, :]
    o_ref[...] = _ln_rows(DN_ALPHA * x_ref[...] + (1.0 + gate2) * acc_ref[...]) * lnw_ref[...] + lnb_ref[...]


def dense_ffn(u2, x1, mod, wg, wu, wd, ln_w, ln_b, seq):
    t, d = x1.shape
    f = wg.shape[1]
    tpb = seq // TM
    full = lambda shp: pl.BlockSpec(shp, lambda i: (0, 0))
    row = lambda n: pl.BlockSpec((TM, n), lambda i: (i, 0))
    return pl.pallas_call(
        _ffn_kernel,
        grid=(t // TM,),
        in_specs=[
            row(d), row(d),
            pl.BlockSpec((1, 6, d), lambda i: (i // tpb, 0, 0)),
            full((d, f)), full((d, f)), full((f, d)), full((1, d)), full((1, d)),
        ],
        out_specs=row(d),
        out_shape=jax.ShapeDtypeStruct((t, d), F32),
        scratch_shapes=[pltpu.VMEM((TM, d), F32)],
        compiler_params=_cparams(("parallel",)),
        name="dense_ffn",
    )(u2, x1, mod, wg.astype(BF16), wu.astype(BF16), wd.astype(BF16), ln_w[None, :], ln_b[None, :])


MOE_WIN = MOE_TILE
MOE_HALF = MOE_WIN // 2


def _router_kernel(u_ref, rw_ref, g_ref, mt_ref, cnt_ref):
    lt = _dot_nt(rw_ref[...], u_ref[...])
    e = lax.broadcasted_iota(I32, lt.shape, 0)
    top1 = jnp.max(lt, axis=0, keepdims=True)
    m1 = e == jnp.min(jnp.where(lt == top1, e, N_EXPERTS), axis=0, keepdims=True)
    l2 = jnp.where(m1, -3.0e38, lt)
    top2 = jnp.max(l2, axis=0, keepdims=True)
    m2 = e == jnp.min(jnp.where(l2 == top2, e, N_EXPERTS), axis=0, keepdims=True)
    e2 = jnp.exp(top2 - top1)
    den = 1.0 + e2
    gates = jnp.where(m1, 1.0 / den, 0.0) + jnp.where(m2, e2 / den, 0.0)
    member = jnp.where(m1 | m2, 1.0, 0.0)
    pad = jnp.zeros((LANE - 2 * N_EXPERTS, lt.shape[1]), F32)
    g_ref[...] = jnp.concatenate([gates, member, pad], axis=0).T
    mt_ref[...] = member
    cnt_ref[0] = jnp.broadcast_to(jnp.sum(member, axis=1, keepdims=True), (N_EXPERTS, LANE))


def moe_router(u2, router_w):
    t, d = u2.shape
    nt = t // MOE_TILE
    return pl.pallas_call(
        _router_kernel,
        grid=(nt,),
        in_specs=[pl.BlockSpec((MOE_TILE, d), lambda i: (i, 0)), pl.BlockSpec((N_EXPERTS, d), lambda i: (0, 0))],
        out_specs=[
            pl.BlockSpec((MOE_TILE, LANE), lambda i: (i, 0)),
            pl.BlockSpec((N_EXPERTS, MOE_TILE), lambda i: (0, i)),
            pl.BlockSpec((1, N_EXPERTS, LANE), lambda i: (i, 0, 0)),
        ],
        out_shape=[
            jax.ShapeDtypeStruct((t, LANE), F32),
            jax.ShapeDtypeStruct((N_EXPERTS, t), F32),
            jax.ShapeDtypeStruct((nt, N_EXPERTS, LANE), F32),
        ],
        compiler_params=_cparams(("parallel",)),
        name="moe_router",
    )(u2, router_w.T.astype(BF16))


def _window_dma(op, hbm_ref, buf_ref, sem_ref, base_ref, cnt_ref, step, slot, to_hbm):
    def copy(e, half):
        rows = pl.ds(pl.multiple_of(base_ref[step * N_EXPERTS + e] + half * MOE_HALF, MOE_ALIGN), MOE_HALF)
        vmem = buf_ref.at[slot, e, pl.ds(half * MOE_HALF, MOE_HALF)]
        src, dst = (vmem, hbm_ref.at[rows]) if to_hbm else (hbm_ref.at[rows], vmem)
        getattr(pltpu.make_async_copy(src, dst, sem_ref.at[slot]), op)()

    for e in range(N_EXPERTS):
        copy(e, 0)
        pl.when(cnt_ref[step * N_EXPERTS + e] > MOE_HALF)(functools.partial(copy, e, 1))


def _dispatch_kernel(base_ref, cnt_ref, u_ref, mt_ref, xs_in_ref, xs_ref, buf_ref, sem_ref):
    del xs_in_ref
    i = pl.program_id(0)
    slot = i % 2
    dma = functools.partial(_window_dma, hbm_ref=xs_ref, buf_ref=buf_ref, sem_ref=sem_ref, base_ref=base_ref,
                            cnt_ref=cnt_ref, to_hbm=True)
    mt = mt_ref[...]
    n = mt.shape[1]
    ri = lax.broadcasted_iota(I32, (n, n), 0)
    ci = lax.broadcasted_iota(I32, (n, n), 1)
    before = jnp.where(ri < ci, 1.0, 0.0).astype(BF16)
    rank = _dot(mt.astype(BF16), before).astype(I32)
    wr = lax.broadcasted_iota(I32, (MOE_HALF, n), 0)

    def gather(e, half):
        hit = (wr + half * MOE_HALF == rank[e:e + 1, :]) & (mt[e:e + 1, :] > 0.0)
        buf_ref[slot, e, half * MOE_HALF:(half + 1) * MOE_HALF] = _dot(
            jnp.where(hit, 1.0, 0.0).astype(BF16), u_ref[...]).astype(BF16)

    for e in range(N_EXPERTS):
        gather(e, 0)
        pl.when(cnt_ref[i * N_EXPERTS + e] > MOE_HALF)(functools.partial(gather, e, 1))

    pl.when(i > 0)(lambda: dma("wait", step=i - 1, slot=1 - slot))
    dma("start", step=i, slot=slot)
    pl.when(i == pl.num_programs(0) - 1)(lambda: dma("wait", step=i, slot=slot))


def moe_dispatch(u2, member_t, base, cnt, n_rows):
    t, d = u2.shape
    nt = t // MOE_TILE
    xs0 = jnp.zeros((n_rows, d), BF16)
    grid_spec = pltpu.PrefetchScalarGridSpec(
        num_scalar_prefetch=2,
        grid=(nt,),
        in_specs=[
            pl.BlockSpec((MOE_TILE, d), lambda i, base, cnt: (i, 0)),
            pl.BlockSpec((N_EXPERTS, MOE_TILE), lambda i, base, cnt: (0, i)),
            pl.BlockSpec(memory_space=pl.ANY),
        ],
        out_specs=pl.BlockSpec(memory_space=pl.ANY),
        scratch_shapes=[
            pltpu.VMEM((2, N_EXPERTS, MOE_WIN, d), BF16),
            pltpu.SemaphoreType.DMA((2,)),
        ],
    )
    return pl.pallas_call(
        _dispatch_kernel,
        grid_spec=grid_spec,
        out_shape=jax.ShapeDtypeStruct((n_rows, d), BF16),
        input_output_aliases={4: 0},
        compiler_params=_cparams(("arbitrary",)),
        name="moe_dispatch",
    )(base.reshape(-1), cnt.reshape(-1), u2, member_t, xs0)


def _expert_kernel(be_ref, act_ref, x_ref, wg_ref, wu_ref, wd_ref, y_ref, acc_ref):
    i = pl.program_id(0)

    @pl.when(act_ref[i] > 0)
    def _():
        x = x_ref[...]
        for k, (f0, fc) in enumerate(_chunks(wg_ref.shape[2], FCHUNK)):
            hid = (_silu(_dot(x, wg_ref[0, :, f0:f0 + fc])) * _dot(x, wu_ref[0, :, f0:f0 + fc])).astype(BF16)
            part = _dot(hid, wd_ref[0, f0:f0 + fc, :])
            if k == 0:
                acc_ref[...] = part
            else:
                acc_ref[...] += part
        y_ref[...] = acc_ref[...].astype(BF16)

    @pl.when(act_ref[i] == 0)
    def _():
        y_ref[...] = jnp.zeros_like(y_ref)


def moe_experts(xs, blk_expert, blk_active, wg, wu, wd):
    n_rows, d = xs.shape
    f = wg.shape[2]
    resident = dict(pipeline_mode=pl.Buffered(1))
    grid_spec = pltpu.PrefetchScalarGridSpec(
        num_scalar_prefetch=2,
        grid=(n_rows // MOE_BLK,),
        in_specs=[
            pl.BlockSpec((MOE_BLK, d), lambda i, be, act: (i, 0)),
            pl.BlockSpec((1, d, f), lambda i, be, act: (be[i], 0, 0), **resident),
            pl.BlockSpec((1, d, f), lambda i, be, act: (be[i], 0, 0), **resident),
            pl.BlockSpec((1, f, d), lambda i, be, act: (be[i], 0, 0), **resident),
        ],
        out_specs=pl.BlockSpec((MOE_BLK, d), lambda i, be, act: (i, 0)),
        scratch_shapes=[pltpu.VMEM((MOE_BLK, d), F32)],
    )
    return pl.pallas_call(
        _expert_kernel,
        grid_spec=grid_spec,
        out_shape=jax.ShapeDtypeStruct((n_rows, d), BF16),
        compiler_params=_cparams(("arbitrary",)),
        name="moe_experts",
    )(blk_expert, blk_active, xs, wg.astype(BF16), wu.astype(BF16), wd.astype(BF16))


def _combine_kernel(base_ref, cnt_ref, g_ref, x_ref, mod_ref, lnw_ref, lnb_ref, y_ref, o_ref,
                    buf_ref, sem_ref, f_ref):
    i = pl.program_id(0)
    slot = i % 2
    dma = functools.partial(_window_dma, hbm_ref=y_ref, buf_ref=buf_ref, sem_ref=sem_ref, base_ref=base_ref,
                            cnt_ref=cnt_ref, to_hbm=False)
    pl.when(i == 0)(lambda: dma("start", step=0, slot=0))
    pl.when(i + 1 < pl.num_programs(0))(lambda: dma("start", step=i + 1, slot=1 - slot))

    gm = g_ref[...]
    n = gm.shape[0]
    ri = lax.broadcasted_iota(I32, (n, n), 0)
    ci = lax.broadcasted_iota(I32, (n, n), 1)
    before = jnp.where(ci < ri, 1.0, 0.0).astype(BF16)
    lane = lax.broadcasted_iota(I32, (1, LANE), 1)
    member = jnp.where((lane >= N_EXPERTS) & (lane < 2 * N_EXPERTS), gm, 0.0)
    rank = _dot(before, member.astype(BF16)).astype(I32)
    wc = lax.broadcasted_iota(I32, (n, MOE_HALF), 1)
    dma("wait", step=i, slot=slot)

    def gathered(e, half):
        le = N_EXPERTS + e
        hit = (wc + half * MOE_HALF == rank[:, le:le + 1]) & (gm[:, le:le + 1] > 0.0)
        rows = buf_ref[slot, e, half * MOE_HALF:(half + 1) * MOE_HALF]
        return gm[:, e:e + 1] * _dot(jnp.where(hit, 1.0, 0.0).astype(BF16), rows)

    f = gathered(0, 0)
    for e in range(1, N_EXPERTS):
        f = f + gathered(e, 0)
    f_ref[...] = f

    def add_upper(e):
        f_ref[...] += gathered(e, 1)

    for e in range(N_EXPERTS):
        pl.when(cnt_ref[i * N_EXPERTS + e] > MOE_HALF)(functools.partial(add_upper, e))
    gate2 = mod_ref[0, 5:6, :]
    o_ref[...] = _ln_rows(DN_ALPHA * x_ref[...] + (1.0 + gate2) * f_ref[...]) * lnw_ref[...] + lnb_ref[...]


def moe_combine(y, gates, base, cnt, x1, mod, ln_w, ln_b, seq):
    t, d = x1.shape
    nt = t // MOE_TILE
    tpb = seq // MOE_TILE
    grid_spec = pltpu.PrefetchScalarGridSpec(
        num_scalar_prefetch=2,
        grid=(nt,),
        in_specs=[
            pl.BlockSpec((MOE_TILE, LANE), lambda i, base, cnt: (i, 0)),
            pl.BlockSpec((MOE_TILE, d), lambda i, base, cnt: (i, 0)),
            pl.BlockSpec((1, 6, d), lambda i, base, cnt: (i // tpb, 0, 0)),
            pl.BlockSpec((1, d), lambda i, base, cnt: (0, 0)),
            pl.BlockSpec((1, d), lambda i, base, cnt: (0, 0)),
            pl.BlockSpec(memory_space=pl.ANY),
        ],
        out_specs=pl.BlockSpec((MOE_TILE, d), lambda i, base, cnt: (i, 0)),
        scratch_shapes=[
            pltpu.VMEM((2, N_EXPERTS, MOE_WIN, d), BF16),
            pltpu.SemaphoreType.DMA((2,)),
            pltpu.VMEM((MOE_TILE, d), F32),
        ],
    )
    return pl.pallas_call(
        _combine_kernel,
        grid_spec=grid_spec,
        out_shape=jax.ShapeDtypeStruct((t, d), F32),
        compiler_params=_cparams(("arbitrary",)),
        name="moe_combine",
    )(base.reshape(-1), cnt.reshape(-1), gates, x1, mod, ln_w[None, :], ln_b[None, :], y)


def moe_ffn(u2, x1, mod, router_w, wg, wu, wd, ln_w, ln_b, seq):
    t, d = u2.shape
    nt = t // MOE_TILE
    gates, member_t, cnt = moe_router(u2, router_w)
    cnt = cnt[:, :, 0].astype(I32)
    seg = (cnt + MOE_ALIGN - 1) // MOE_ALIGN * MOE_ALIGN
    region = (jnp.sum(seg, axis=0) + MOE_HALF + MOE_BLK - 1) // MOE_BLK * MOE_BLK
    region_end = jnp.cumsum(region)
    base = ((region_end - region)[None, :] + jnp.cumsum(seg, axis=0) - seg).astype(I32)
    n_blk = -(-(2 * t + nt * N_EXPERTS * (MOE_ALIGN - 1) + N_EXPERTS * (MOE_HALF + MOE_BLK - 1)) // MOE_BLK)
    blk_start = jnp.arange(n_blk, dtype=I32) * MOE_BLK
    blk_expert = jnp.minimum(jnp.sum((blk_start[:, None] >= region_end[None, :]).astype(I32), axis=1), N_EXPERTS - 1)
    blk_active = (blk_start < region_end[-1]).astype(I32)
    xs = moe_dispatch(u2, member_t, base, cnt, n_blk * MOE_BLK)
    y = moe_experts(xs, blk_expert, blk_active, wg, wu, wd)
    return moe_combine(y, gates, base, cnt, x1, mod, ln_w, ln_b, seq)


def _mixer_front(x, mod, w_in, dt_bias, cmp_pos_k, cmp_w1_k, cmp_w2_k, cmp_pos_v, cmp_w1_v, cmp_w2_v):
    b, s, d = x.shape
    t = b * s
    g = N_KV_HEADS
    q, kv, z, xbc, sm = in_proj(x.reshape(t, d), mod, w_in, dt_bias, s)
    kv = kv.reshape(b, s, 6, 128)
    kvc = compress(jnp.stack([kv[:, :, 0], kv[:, :, 1]]),
                   jnp.stack([cmp_pos_k, cmp_pos_v]), jnp.stack([cmp_w1_k, cmp_w1_v]),
                   jnp.stack([cmp_w2_k, cmp_w2_v]))
    heads = lambda a: a.reshape(b, -1, g, HEAD_DIM).transpose(0, 2, 1, 3)
    gates = sm[:, :GATE_COLS].reshape(b, s, g, 3 * ATT_GQA).transpose(0, 2, 3, 1)
    att = nsa_attention(q.reshape(b, s, ATT_WIDTH), gates, heads(kvc[0]), heads(kvc[1]),
                        heads(kv[:, :, 2]), heads(kv[:, :, 3]), heads(kv[:, :, 4]), heads(kv[:, :, 5]))
    return att, z, xbc, sm


def kernel(x, c, ada_w, ada_b, w_in, cmp_pos_k, cmp_w1_k, cmp_w2_k, cmp_pos_v, cmp_w1_v, cmp_w2_v, attn_norm_w, conv_w, conv_b, dt_bias, a_log, d_skip, ssm_norm_w, w_out, ln1_w, ln1_b, ln2_w, ln2_b, ffn_w_gate, ffn_w_up, ffn_w_down, router_w, exp_w_gate, exp_w_up, exp_w_down):
    b, s, d = x.shape
    t = b * s
    mod = ada_mod(c, ada_w, ada_b).reshape(DEPTH, b, 6, d)
    xc = x.reshape(t, d)
    for l in range(DEPTH):
        att, z, xbc, sm = _mixer_front(xc.reshape(b, s, d), mod[l], w_in[l], dt_bias[l],
                                       cmp_pos_k[l], cmp_w1_k[l], cmp_w2_k[l],
                                       cmp_pos_v[l], cmp_w1_v[l], cmp_w2_v[l])
        ssm = ssd_mixer(xbc, z, sm, conv_w[l], conv_b[l], a_log[l], d_skip[l], ssm_norm_w[l], b, s)
        x1, u2 = out_proj(att.reshape(t, ATT_WIDTH), ssm.reshape(t, SSM_WIDTH), xc, mod[l], attn_norm_w[l],
                          w_out[l], ln1_w[l], ln1_b[l], s)
        i = l // 2
        if l % 2 == 0:
            xc = dense_ffn(u2, x1, mod[l], ffn_w_gate[i], ffn_w_up[i], ffn_w_down[i], ln2_w[l], ln2_b[l], s)
        else:
            xc = moe_ffn(u2, x1, mod[l], router_w[i], exp_w_gate[i], exp_w_up[i], exp_w_down[i],
                         ln2_w[l], ln2_b[l], s)
    return xc.reshape(b, s, d)
```

```python
import functools
import math

import numpy as np
import jax
import jax.numpy as jnp
from jax import lax
from jax.experimental import pallas as pl
from jax.experimental.pallas import tpu as pltpu

F32 = jnp.float32
BF16 = jnp.bfloat16
I32 = jnp.int32

D_MODEL = 1024
HEAD_DIM = 64
ATT_WIDTH = 512
N_KV_HEADS = 2
ATT_GQA = 4
CMP_BLOCK = 32
CMP_STRIDE = 16
CMP_HIDDEN = 256
SEL_BLOCK = 64
SEL_TOPN = 16
SEL_FORCE_BONUS = 1e4
WINDOW = 512
SSM_WIDTH = 512
N_SSM_HEADS = 8
SSM_STATE = 128
SSM_CONV = 4
SSM_CHUNK = 128
FFN_DENSE = 2816
N_EXPERTS = 8
FFN_EXPERT = 3584
DEPTH = 2
DN_ALPHA = (2 * DEPTH) ** 0.25
LN_EPS = 1e-5
RMS_EPS = 1e-6
NEG = -1e30

LANE = 128
VMEM_LIMIT = 56 * 1024 * 1024

TM = 512
TQ = 512
TK = 512
MOE_TILE = 512
MOE_ALIGN = 16
MOE_BLK = 1024
FCHUNK = 512


def _cparams(sem):
    return pltpu.CompilerParams(dimension_semantics=sem, vmem_limit_bytes=VMEM_LIMIT)


def _dot(a, b):
    return jnp.dot(a, b, preferred_element_type=F32)


def _dot_nt(a, b):
    return lax.dot_general(a, b, (((1,), (1,)), ((), ())), preferred_element_type=F32)


def _split2(x):
    hi = x.astype(BF16)
    lo = (x - hi.astype(F32)).astype(BF16)
    return hi, lo


def _split3(x):
    hi = x.astype(BF16)
    r = x - hi.astype(F32)
    mid = r.astype(BF16)
    lo = (r - mid.astype(F32)).astype(BF16)
    return hi, mid, lo


def _sigmoid(x):
    return 1.0 / (1.0 + jnp.exp(-x))


def _silu(x):
    return x * _sigmoid(x)


def _ln_rows(x):
    mu = jnp.mean(x, axis=-1, keepdims=True)
    xc = x - mu
    var = jnp.mean(xc * xc, axis=-1, keepdims=True)
    return xc * lax.rsqrt(var + LN_EPS)


def _ada_kernel(c_ref, w_ref, b_ref, o_ref):
    cond = _silu(c_ref[...])
    ch, cl = _split2(cond)
    wh, wl = _split2(w_ref[0])
    o_ref[0] = _dot(ch, wh) + _dot(cl, wh) + _dot(ch, wl) + b_ref[0]


def ada_mod(c, ada_w, ada_b):
    nl, d, n = ada_w.shape
    b = c.shape[0]
    tn = 1536
    return pl.pallas_call(
        _ada_kernel,
        grid=(nl, n // tn),
        in_specs=[
            pl.BlockSpec((b, d), lambda l, j: (0, 0)),
            pl.BlockSpec((1, d, tn), lambda l, j: (l, 0, j)),
            pl.BlockSpec((1, 1, tn), lambda l, j: (l, 0, j)),
        ],
        out_specs=pl.BlockSpec((1, b, tn), lambda l, j: (l, 0, j)),
        out_shape=jax.ShapeDtypeStruct((nl, b, n), F32),
        compiler_params=_cparams(("parallel", "parallel")),
        name="ada_mod",
    )(c, ada_w, ada_b.reshape(nl, 1, n))


KV_COLS = 6 * 128
GATE_COLS = 24
DT_LANE0 = 32


def _inproj_kernel(x_ref, mod_ref, wq_ref, wkv_ref, wz_ref, wx_ref, wsh_ref, wsl_ref, dtb_ref,
                   q_ref, kv_ref, z_ref, xbc_ref, sm_ref):
    xn = _ln_rows(x_ref[...])
    shift = mod_ref[0, 0:1, :]
    scale = mod_ref[0, 1:2, :]
    u = xn * (1.0 + scale) + shift
    ub = u.astype(BF16)
    ul = (u - ub.astype(F32)).astype(BF16)
    q_ref[...] = (_dot(ub, wq_ref[...]) * (LOG2E * HEAD_DIM ** -0.5)).astype(BF16)
    kv_ref[...] = _dot(ub, wkv_ref[...]).astype(BF16)
    z_ref[...] = _dot(ub, wz_ref[...])
    xbc_ref[...] = _dot(ub, wx_ref[...])
    sm = _dot(ub, wsh_ref[...]) + _dot(ul, wsh_ref[...]) + _dot(ub, wsl_ref[...])
    lane = lax.broadcasted_iota(I32, sm.shape, 1)
    xdt = sm + dtb_ref[...]
    dt = jnp.maximum(xdt, 0.0) + jnp.log(1.0 + jnp.exp(-jnp.abs(xdt)))
    sm_ref[...] = jnp.where(lane < GATE_COLS, _sigmoid(sm), dt)


def in_proj(x2d, mod, w_in, dt_bias, seq):
    t, d = x2d.shape
    wq = w_in[:, :512].astype(BF16)
    wkv = w_in[:, 512:512 + KV_COLS].astype(BF16)
    o = 512 + KV_COLS
    wg = w_in[:, o:o + GATE_COLS]
    wz = w_in[:, o + GATE_COLS:o + GATE_COLS + 512].astype(BF16)
    o2 = o + GATE_COLS + 512
    wx = w_in[:, o2:o2 + 1024].astype(BF16)
    wdt = w_in[:, o2 + 1024:]
    wsm = jnp.zeros((d, LANE), F32).at[:, :GATE_COLS].set(wg).at[:, DT_LANE0:DT_LANE0 + N_SSM_HEADS].set(wdt)
    wsh, wsl = _split2(wsm)
    dtb = jnp.zeros((1, LANE), F32).at[0, DT_LANE0:DT_LANE0 + N_SSM_HEADS].set(dt_bias)
    tpb = seq // TM
    full = lambda shp: pl.BlockSpec(shp, lambda i: (0, 0))
    row = lambda n: pl.BlockSpec((TM, n), lambda i: (i, 0))
    return pl.pallas_call(
        _inproj_kernel,
        grid=(t // TM,),
        in_specs=[
            row(d),
            pl.BlockSpec((1, 6, d), lambda i: (i // tpb, 0, 0)),
            full((d, 512)), full((d, KV_COLS)), full((d, 512)), full((d, 1024)),
            full((d, LANE)), full((d, LANE)), full((1, LANE)),
        ],
        out_specs=[row(512), row(KV_COLS), row(512), row(1024), row(LANE)],
        out_shape=[
            jax.ShapeDtypeStruct((t, 512), BF16),
            jax.ShapeDtypeStruct((t, KV_COLS), BF16),
            jax.ShapeDtypeStruct((t, 512), F32),
            jax.ShapeDtypeStruct((t, 1024), F32),
            jax.ShapeDtypeStruct((t, LANE), F32),
        ],
        compiler_params=_cparams(("parallel",)),
        name="in_proj",
    )(x2d, mod, wq, wkv, wz, wx, wsh, wsl, dtb)


def _gelu_tanh(x):
    return 0.5 * x * (1.0 + jnp.tanh(math.sqrt(2.0 / math.pi) * (x + 0.044715 * (x * x * x))))


def _cmp_kernel(x_ref, w1e_ref, w1_ref, pos_ref, w2e_ref, o_ref):
    x = x_ref[0, 0]
    nc = x.shape[0]
    ab = _dot(x, w1e_ref[0])
    half = N_KV_HEADS * CMP_HIDDEN
    first = ab[:, :half]
    second = pltpu.roll(ab[:, half:], nc - 1, 0)
    ph, plo = _split2(pos_ref[0])
    pb = (_dot(ph, w1_ref[0]) + _dot(plo, w1_ref[0]))[0:1, :]
    pb = jnp.concatenate([pb] * N_KV_HEADS, axis=1)
    h = _gelu_tanh(first + second + pb)
    o_ref[0, 0] = _dot(h.astype(BF16), w2e_ref[0]).astype(BF16)


def compress(kv_raw, pos, w1, w2):
    _, b, s, _ = kv_raw.shape
    nc = s // CMP_STRIDE
    x = kv_raw.reshape(2, b, nc, CMP_STRIDE * 128)
    g = N_KV_HEADS
    w1r = w1.reshape(2, 2, CMP_STRIDE, HEAD_DIM, CMP_HIDDEN)
    eye = jnp.eye(g, dtype=F32)
    w1e = jnp.einsum('kjtdh,ga->ktgdjah', w1r, eye).reshape(2, CMP_STRIDE * g * HEAD_DIM, 2 * g * CMP_HIDDEN)
    w2e = jnp.einsum('khd,ga->kghad', w2, eye).reshape(2, g * CMP_HIDDEN, g * HEAD_DIM)
    posf = jnp.zeros((2, 8, CMP_BLOCK * HEAD_DIM), F32).at[:, 0, :].set(pos.reshape(2, -1))
    out = pl.pallas_call(
        _cmp_kernel,
        grid=(2, b),
        in_specs=[
            pl.BlockSpec((1, 1, nc, CMP_STRIDE * 128), lambda k, i: (k, i, 0, 0)),
            pl.BlockSpec((1,) + w1e.shape[1:], lambda k, i: (k, 0, 0)),
            pl.BlockSpec((1,) + w1.shape[1:], lambda k, i: (k, 0, 0)),
            pl.BlockSpec((1,) + posf.shape[1:], lambda k, i: (k, 0, 0)),
            pl.BlockSpec((1,) + w2e.shape[1:], lambda k, i: (k, 0, 0)),
        ],
        out_specs=pl.BlockSpec((1, 1, nc, 128), lambda k, i: (k, i, 0, 0)),
        out_shape=jax.ShapeDtypeStruct((2, b, nc, 128), BF16),
        compiler_params=_cparams(("parallel", "parallel")),
        name="compress",
    )(x, w1e.astype(BF16), w1.astype(BF16), posf, w2e.astype(BF16))
    return out


R_ROWS = ATT_GQA * TQ
WIN_KEYS = WINDOW + TQ
KAUG = LANE + HEAD_DIM
VROWS = HEAD_DIM + 16
LOG2E = math.log2(math.e)
SEL_PAD = TK - TQ
SEL_SPLIT = 4
PRE_SPLIT = 1


def _exp_cols(s):
    return jnp.exp2(s - jnp.max(s, axis=0, keepdims=True))


def _sel_scores(k0, width, qaug_ref, kaug_ref):
    return _dot(kaug_ref[0, 0, pl.ds(k0, width), :], qaug_ref[...])


def _sel_update(st, k0, m_old, vt_ref, acc_ref):
    m_new = jnp.maximum(m_old, jnp.max(st, axis=0, keepdims=True))
    pt = jnp.exp2(st - m_new).astype(BF16)
    c0 = k0 // LANE
    vt = jnp.concatenate([vt_ref[0, 0, c0 + c] for c in range(st.shape[0] // LANE)], axis=1)
    acc_ref[...] = jnp.exp2(m_old - m_new) * acc_ref[...] + _dot(vt, pt)
    return m_new


def _sel_update_cols(st, k0, m_old, vt_ref, acc_ref, ls):
    m_new = jnp.maximum(m_old, jnp.max(st, axis=0, keepdims=True))
    pt = jnp.exp2(st - m_new).astype(BF16)
    c0 = k0 // LANE
    vt = jnp.concatenate([vt_ref[0, 0, c0 + c] for c in range(st.shape[0] // LANE)], axis=1)
    acc_ref[:, ls] = jnp.exp2(m_old - m_new) * acc_ref[:, ls] + _dot(vt, pt)
    return m_new


def _attn_kernel(q_ref, gate_ref, kc_ref, vct_ref, ovt_ref, cband_ref, wband_ref, kaug_ref, vt_ref, kw_ref, vwt_ref,
                 o_ref, qaug_ref, acc_ref, s0_ref, s1_ref, part_ref):
    i = pl.program_id(2)
    t0 = i * TQ
    qt = q_ref[0].astype(F32).T.astype(BF16)
    qst = jnp.concatenate([qt[r * HEAD_DIM:(r + 1) * HEAD_DIM, :] for r in range(ATT_GQA)], axis=1)
    t_lane = t0 + lax.broadcasted_iota(I32, (1, R_ROWS), 1) % TQ

    kc = kc_ref[0, 0]
    nc = kc.shape[0]
    hpg = ATT_GQA // PRE_SPLIT
    cband = jnp.concatenate([cband_ref[pl.ds(pl.multiple_of(nc - t0 // CMP_STRIDE, 8), nc), :]] * hpg, axis=1)
    ws = pl.multiple_of(jnp.maximum(t0 - WINDOW, 0), TQ)
    wband = jnp.concatenate(
        [wband_ref[pl.ds(pl.multiple_of(WINDOW - (t0 - ws), TQ), WIN_KEYS), :]] * hpg, axis=1)
    cw = ws // LANE
    gt = gate_ref[0, 0]
    psum = None
    for grp in range(PRE_SPLIT):
        ls = slice(grp * hpg * TQ, (grp + 1) * hpg * TQ)
        qg = qst[:, ls]
        e = _exp_cols(_dot(kc, qg) + cband)
        p = e * jnp.where(t_lane[:, ls] >= CMP_BLOCK - 1, 1.0 / jnp.sum(e, axis=0, keepdims=True), 0.0)
        o_cmp = _dot(vct_ref[0, 0], p.astype(BF16))
        ew = _exp_cols(_dot(kw_ref[0, 0, pl.ds(ws, WIN_KEYS), :], qg) + wband).astype(BF16)
        o_win = _dot(vwt_ref[0, 0, cw], ew[0:LANE])
        for c in range(1, WIN_KEYS // LANE):
            o_win = o_win + _dot(vwt_ref[0, 0, cw + c], ew[c * LANE:(c + 1) * LANE])
        o_win = o_win[:HEAD_DIM] * (1.0 / o_win[HEAD_DIM:HEAD_DIM + 1])
        for hh in range(hpg):
            r = grp * hpg + hh
            sl = slice(hh * TQ, (hh + 1) * TQ)
            part_ref[r * HEAD_DIM:(r + 1) * HEAD_DIM, :] = (gt[3 * r:3 * r + 1] * o_cmp[:, sl]
                                                            + gt[3 * r + 2:3 * r + 3] * o_win[:, sl])
            psum = p[:, sl] if psum is None else psum + p[:, sl]

    ph, plo = _split2(psum)
    ovt = ovt_ref[...]
    imp = _dot(ovt, ph) + _dot(ovt, plo)
    j = lax.broadcasted_iota(I32, (LANE, TQ), 0)
    jt = (t0 + lax.broadcasted_iota(I32, (LANE, TQ), 1)) // SEL_BLOCK
    forced = (j == 0) | (j == jt) | (j == jt - 1)
    valid = j <= jt
    score = jnp.where(forced, NEG, jnp.where(valid, imp, NEG))

    def pick(_, carry):
        sc, sel = carry
        mx = jnp.max(sc, axis=0, keepdims=True)
        idx = jnp.min(jnp.where(sc == mx, j, LANE), axis=0, keepdims=True)
        hit = j == idx
        return jnp.where(hit, -3.0e38, sc), jnp.where(hit, 1.0, sel)

    _, sel = lax.fori_loop(0, SEL_TOPN - 3, pick, (score, jnp.where(forced, 1.0, 0.0)), unroll=True)
    bias_t = jnp.where(valid, jnp.where(sel > 0.0, 0.0, NEG), NEG).astype(BF16)
    for r in range(ATT_GQA):
        qaug_ref[0:LANE, r * TQ:(r + 1) * TQ] = bias_t
    qaug_ref[LANE:KAUG, :] = qst

    scores = functools.partial(_sel_scores, qaug_ref=qaug_ref, kaug_ref=kaug_ref)
    update = functools.partial(_sel_update, vt_ref=vt_ref, acc_ref=acc_ref)
    acc_ref[...] = jnp.zeros(acc_ref.shape, F32)
    kd = pl.multiple_of(t0 + SEL_PAD, TQ)
    kpos = t0 + lax.broadcasted_iota(I32, (TQ, 1), 0)
    m = update(jnp.where(kpos <= t_lane, scores(kd, TQ), NEG), kd, jnp.full((1, R_ROWS), NEG, F32))
    n_t = (t0 + TK - 1) // TK
    kstart = lambda jj: pl.multiple_of(t0 + SEL_PAD - (jj + 1) * TK, TQ)

    @pl.when(n_t > 0)
    def _():
        s0_ref[...] = scores(kstart(0), TK)

    def pair(pp, mm):
        ka, kb, kc2 = kstart(2 * pp), kstart(jnp.minimum(2 * pp + 1, n_t - 1)), kstart(jnp.minimum(2 * pp + 2, n_t - 1))
        kb_u = kstart(2 * pp + 1)
        outs = []
        for hl in range(SEL_SPLIT):
            ls = slice(hl * (R_ROWS // SEL_SPLIT), (hl + 1) * (R_ROWS // SEL_SPLIT))
            s1_ref[:, ls] = _dot(kaug_ref[0, 0, pl.ds(kb, TK), :], qaug_ref[:, ls])
            mh = _sel_update_cols(s0_ref[:, ls], ka, mm[:, ls], vt_ref, acc_ref, ls)
            s0_ref[:, ls] = _dot(kaug_ref[0, 0, pl.ds(kc2, TK), :], qaug_ref[:, ls])
            outs.append(_sel_update_cols(s1_ref[:, ls], kb_u, mh, vt_ref, acc_ref, ls))
        return jnp.concatenate(outs, axis=1)

    m = lax.fori_loop(0, n_t // 2, pair, m)

    @pl.when(n_t % 2 == 1)
    def _():
        update(s0_ref[...], kstart(n_t - 1), m)

    acc = acc_ref[...]
    o_sel = acc[:HEAD_DIM] * (1.0 / acc[HEAD_DIM:HEAD_DIM + 1])

    outs = []
    for r in range(ATT_GQA):
        sl = slice(r * TQ, (r + 1) * TQ)
        outs.append(part_ref[r * HEAD_DIM:(r + 1) * HEAD_DIM, :] + gt[3 * r + 1:3 * r + 2] * o_sel[:, sl])
    o_ref[0] = jnp.concatenate(outs, axis=0).T


def nsa_attention(q, gates, kc, vc, ks, vs, kw, vw):
    b, s, _ = q.shape
    g = N_KV_HEADS
    nc = kc.shape[2]
    n_sel = s // SEL_BLOCK
    n_chunk = s // LANE
    assert SEL_TOPN <= n_sel <= LANE and s % TK == 0 and s >= WIN_KEYS
    chunks_t = lambda a: a.reshape(b, g, -1, LANE, a.shape[-1]).swapaxes(3, 4)
    front = lambda a, fill: jnp.concatenate(
        [jnp.broadcast_to(fill, (b, g, SEL_PAD, a.shape[-1])).astype(a.dtype), a], axis=2)
    cs = np.arange(nc)[None, :] * CMP_STRIDE
    js = np.arange(LANE)[:, None] * SEL_BLOCK
    ovt = ((cs < js + SEL_BLOCK) & (cs + CMP_BLOCK > js) & (np.arange(nc)[None, :] < nc - 1)
           & (np.arange(LANE)[:, None] < n_sel))
    ovt = jnp.asarray(ovt, BF16)
    onehot = jnp.asarray(np.arange(s)[:, None] // SEL_BLOCK == np.arange(LANE)[None, :], BF16)
    kaug = jnp.concatenate([jnp.broadcast_to(onehot, (b, g, s, LANE)), ks], axis=-1)
    pad_row = jnp.asarray(np.concatenate([np.ones(LANE), np.zeros(HEAD_DIM)]), BF16)
    kaug = front(kaug, pad_row)
    ones = jnp.broadcast_to(jnp.zeros((s, VROWS - HEAD_DIM), BF16).at[:, 0].set(1), (b, g, s, VROWS - HEAD_DIM))
    vt = chunks_t(front(jnp.concatenate([vs, ones], axis=-1), jnp.zeros((), BF16)))
    vwt = chunks_t(jnp.concatenate([vw, ones], axis=-1))
    tok = np.arange(TQ)[None, :]
    xc = np.arange(2 * nc)[:, None]
    cband = jnp.asarray(np.where(CMP_STRIDE * (xc - nc) + CMP_BLOCK - 1 <= tok, 0.0, NEG), F32)
    xw = np.arange(WINDOW + WIN_KEYS)[:, None]
    wband = jnp.asarray(np.where((xw <= WINDOW + tok) & (xw > tok), 0.0, NEG), F32)
    const = lambda a: pl.BlockSpec(a.shape, lambda bi, gi, i: (0, 0))
    per_bg = lambda *shp: pl.BlockSpec((1, 1) + shp, lambda bi, gi, i: (bi, gi) + (0,) * len(shp))
    return pl.pallas_call(
        _attn_kernel,
        grid=(b, g, s // TQ),
        in_specs=[
            pl.BlockSpec((1, TQ, ATT_GQA * HEAD_DIM), lambda bi, gi, i: (bi, i, gi)),
            pl.BlockSpec((1, 1, 3 * ATT_GQA, TQ), lambda bi, gi, i: (bi, gi, 0, i)),
            per_bg(nc, HEAD_DIM), per_bg(HEAD_DIM, nc),
            const(ovt), const(cband), const(wband),
            per_bg(s + SEL_PAD, KAUG), per_bg(n_chunk + SEL_PAD // LANE, VROWS, LANE),
            per_bg(s, HEAD_DIM), per_bg(n_chunk, VROWS, LANE),
        ],
        out_specs=pl.BlockSpec((1, TQ, ATT_GQA * HEAD_DIM), lambda bi, gi, i: (bi, i, gi)),
        out_shape=jax.ShapeDtypeStruct((b, s, ATT_WIDTH), F32),
        scratch_shapes=[
            pltpu.VMEM((KAUG, R_ROWS), BF16),
            pltpu.VMEM((VROWS, R_ROWS), F32),
            pltpu.VMEM((TK, R_ROWS), F32),
            pltpu.VMEM((TK, R_ROWS), F32),
            pltpu.VMEM((ATT_GQA * HEAD_DIM, TQ), F32),
        ],
        compiler_params=_cparams(("parallel", "parallel", "arbitrary")),
        name="nsa_attention",
    )(q, gates, kc, vc.swapaxes(2, 3), ovt, cband, wband, kaug, vt, kw, vwt)


def _ssd_kernel(xc_ref, xp_ref, z_ref, sm_ref, cw_ref, cb_ref, alog_ref, dsk_ref, nw_ref, exp_ref, o_ref, st_ref):
    c = pl.program_id(1)

    @pl.when(c == 0)
    def _():
        st_ref[...] = jnp.zeros_like(st_ref)

    q = SSM_CHUNK
    xcur = xc_ref[0]
    xtail = xp_ref[0] * (c > 0).astype(F32)
    row = lax.broadcasted_iota(I32, (8, 1), 0)
    cw = cw_ref[...]
    acc = cb_ref[...] + cw[SSM_CONV - 1:SSM_CONV, :] * xcur
    for s in range(1, SSM_CONV):
        rolled = pltpu.roll(xcur, s, 0)
        head = jnp.where(row >= s, rolled[0:8], pltpu.roll(xtail, s, 0))
        acc = acc + cw[SSM_CONV - 1 - s:SSM_CONV - s, :] * jnp.concatenate([head, rolled[8:]], axis=0)
    xc = _silu(acc)
    xs = xc[:, :SSM_WIDTH]
    bm = xc[:, SSM_WIDTH:SSM_WIDTH + 2 * SSM_STATE]
    cm = xc[:, SSM_WIDTH + 2 * SSM_STATE:]

    sm = sm_ref[0]
    lane = lax.broadcasted_iota(I32, (1, LANE), 1)
    head_lane = (lane >= DT_LANE0) & (lane < DT_LANE0 + N_SSM_HEADS)
    da = jnp.where(head_lane, sm * (-jnp.exp(alog_ref[...])), 0.0)
    ri = lax.broadcasted_iota(I32, (q, q), 0)
    ci = lax.broadcasted_iota(I32, (q, q), 1)
    causal = ri >= ci
    tri = jnp.where(causal, 1.0, 0.0).astype(BF16)
    d1, d2, d3 = _split3(da)
    acum = _dot(tri, d1) + _dot(tri, d2) + _dot(tri, d3)
    acum_t = acum.T
    last = acum[q - 1:q, :]
    eac = jnp.exp(acum)
    to_end = jnp.exp(last - acum)
    elast = jnp.exp(last)

    def spread(v):
        hi, lo = _split2(v)
        return _dot(hi, exp_ref[...]) + _dot(lo, exp_ref[...])

    xdt = xs * spread(sm)
    xdt_b = xdt.astype(BF16)
    xte_b = (xdt * spread(to_end)).astype(BF16)
    eac_w = spread(eac)
    elast_w = spread(jnp.broadcast_to(elast, (8, LANE)))[0:1]

    hpg = N_SSM_HEADS // 2
    gw = hpg * HEAD_DIM
    ys = []
    for g in range(2):
        bg = bm[:, g * SSM_STATE:(g + 1) * SSM_STATE]
        cg = cm[:, g * SSM_STATE:(g + 1) * SSM_STATE].astype(BF16)
        gmat = _dot_nt(cg, bg.astype(BF16))
        bgt = bg.T.astype(BF16)
        st = st_ref[g]
        ys.append(eac_w[:, g * gw:(g + 1) * gw] * _dot(cg, st.astype(BF16)))
        st_ref[g] = elast_w[:, g * gw:(g + 1) * gw] * st + _dot(bgt, xte_b[:, g * gw:(g + 1) * gw])
        for hh in range(hpg):
            h = g * hpg + hh
            ln = DT_LANE0 + h
            dec = jnp.exp(jnp.where(causal, acum[:, ln:ln + 1] - acum_t[ln:ln + 1, :], NEG))
            ys.append(_dot((gmat * dec).astype(BF16), xdt_b[:, h * HEAD_DIM:(h + 1) * HEAD_DIM]))
    intra = [jnp.concatenate(ys[g * (hpg + 1) + 1:(g + 1) * (hpg + 1)], axis=1) for g in range(2)]
    y = (jnp.concatenate(intra, axis=1) + jnp.concatenate([ys[0], ys[hpg + 1]], axis=1)
         + dsk_ref[...] * xs)
    yz = y * _silu(z_ref[0])
    half = SSM_WIDTH // 2
    outs = []
    for grp in range(2):
        v = yz[:, grp * half:(grp + 1) * half]
        outs.append(v * lax.rsqrt(jnp.mean(v * v, axis=-1, keepdims=True) + RMS_EPS)
                    * nw_ref[:, grp * half:(grp + 1) * half])
    o_ref[0] = jnp.concatenate(outs, axis=1).astype(BF16)


def ssd_mixer(xbc, z, sm, conv_w, conv_b, a_log, d_skip, ssm_norm_w, b, s):
    q = SSM_CHUNK
    cdim = xbc.shape[-1]
    alog = jnp.zeros((1, LANE), F32).at[0, DT_LANE0:DT_LANE0 + N_SSM_HEADS].set(a_log)
    dsk = jnp.repeat(d_skip, HEAD_DIM)[None, :]
    spread = np.zeros((LANE, SSM_WIDTH), np.float32)
    for h in range(N_SSM_HEADS):
        spread[DT_LANE0 + h, h * HEAD_DIM:(h + 1) * HEAD_DIM] = 1.0
    blk = lambda w: pl.BlockSpec((1, q, w), lambda bi, c: (bi, c, 0))
    full = lambda shp: pl.BlockSpec(shp, lambda bi, c: (0, 0))
    tail_rows = 8
    return pl.pallas_call(
        _ssd_kernel,
        grid=(b, s // q),
        in_specs=[
            blk(cdim),
            pl.BlockSpec((1, tail_rows, cdim), lambda bi, c: (bi, jnp.maximum(c * (q // tail_rows) - 1, 0), 0)),
            blk(SSM_WIDTH), blk(LANE),
            full((SSM_CONV, cdim)), full((1, cdim)), full((1, LANE)), full((1, SSM_WIDTH)), full((1, SSM_WIDTH)),
            full((LANE, SSM_WIDTH)),
        ],
        out_specs=blk(SSM_WIDTH),
        out_shape=jax.ShapeDtypeStruct((b, s, SSM_WIDTH), BF16),
        scratch_shapes=[pltpu.VMEM((2, SSM_STATE, SSM_WIDTH // 2), F32)],
        compiler_params=_cparams(("parallel", "arbitrary")),
        name="ssd_mixer",
    )(xbc.reshape(b, s, cdim), xbc.reshape(b, s, cdim), z.reshape(b, s, SSM_WIDTH), sm.reshape(b, s, LANE),
      conv_w, conv_b[None, :], alog, dsk, ssm_norm_w[None, :], jnp.asarray(spread, BF16))


def _outproj_kernel(att_ref, ssm_ref, x_ref, mod_ref, anw_ref, wa_ref, ws_ref, lnw_ref, lnb_ref,
                    x1_ref, u2_ref):
    att = att_ref[...]
    an = att * lax.rsqrt(jnp.mean(att * att, axis=-1, keepdims=True) + RMS_EPS) * anw_ref[...]
    h = _dot(an.astype(BF16), wa_ref[...]) + _dot(ssm_ref[...], ws_ref[...])
    gate1 = mod_ref[0, 2:3, :]
    x1 = _ln_rows(DN_ALPHA * x_ref[...] + (1.0 + gate1) * h) * lnw_ref[...] + lnb_ref[...]
    x1_ref[...] = x1
    shift2 = mod_ref[0, 3:4, :]
    scale2 = mod_ref[0, 4:5, :]
    u2_ref[...] = (_ln_rows(x1) * (1.0 + scale2) + shift2).astype(BF16)


def out_proj(att, ssm, x2d, mod, attn_norm_w, w_out, ln_w, ln_b, seq):
    t, d = x2d.shape
    tpb = seq // TM
    wo = w_out.astype(BF16)
    full = lambda shp: pl.BlockSpec(shp, lambda i: (0, 0))
    row = lambda n: pl.BlockSpec((TM, n), lambda i: (i, 0))
    return pl.pallas_call(
        _outproj_kernel,
        grid=(t // TM,),
        in_specs=[
            row(ATT_WIDTH), row(SSM_WIDTH), row(d),
            pl.BlockSpec((1, 6, d), lambda i: (i // tpb, 0, 0)),
            full((1, ATT_WIDTH)), full((ATT_WIDTH, d)), full((SSM_WIDTH, d)), full((1, d)), full((1, d)),
        ],
        out_specs=[row(d), row(d)],
        out_shape=[jax.ShapeDtypeStruct((t, d), F32), jax.ShapeDtypeStruct((t, d), BF16)],
        compiler_params=_cparams(("parallel",)),
        name="out_proj",
    )(att, ssm, x2d, mod, attn_norm_w[None, :], wo[:ATT_WIDTH], wo[ATT_WIDTH:], ln_w[None, :], ln_b[None, :])


def _chunks(n, c):
    return [(f0, min(c, n - f0)) for f0 in range(0, n, c)]


def _ffn_kernel(u_ref, x_ref, mod_ref, wg_ref, wu_ref, wd_ref, lnw_ref, lnb_ref, o_ref, acc_ref):
    u = u_ref[...]
    for k, (f0, fc) in enumerate(_chunks(wg_ref.shape[1], FCHUNK)):
        hid = (_silu(_dot(u, wg_ref[:, f0:f0 + fc])) * _dot(u, wu_ref[:, f0:f0 + fc])).astype(BF16)
        part = _dot(hid, wd_ref[f0:f0 + fc, :])
        if k == 0:
            acc_ref[...] = part
        else:
            acc_ref[...] += part
    gate2 = mod_ref[0, 5:6, :]
    o_ref[...] = _ln_rows(DN_ALPHA * x_ref[...] + (1.0 + gate2) * acc_ref[...]) * lnw_ref[...] + lnb_ref[...]


def dense_ffn(u2, x1, mod, wg, wu, wd, ln_w, ln_b, seq):
    t, d = x1.shape
    f = wg.shape[1]
    tpb = seq // TM
    full = lambda shp: pl.BlockSpec(shp, lambda i: (0, 0))
    row = lambda n: pl.BlockSpec((TM, n), lambda i: (i, 0))
    return pl.pallas_call(
        _ffn_kernel,
        grid=(t // TM,),
        in_specs=[
            row(d), row(d),
            pl.BlockSpec((1, 6, d), lambda i: (i // tpb, 0, 0)),
            full((d, f)), full((d, f)), full((f, d)), full((1, d)), full((1, d)),
        ],
        out_specs=row(d),
        out_shape=jax.ShapeDtypeStruct((t, d), F32),
        scratch_shapes=[pltpu.VMEM((TM, d), F32)],
        compiler_params=_cparams(("parallel",)),
        name="dense_ffn",
    )(u2, x1, mod, wg.astype(BF16), wu.astype(BF16), wd.astype(BF16), ln_w[None, :], ln_b[None, :])


MOE_WIN = MOE_TILE
MOE_HALF = MOE_WIN // 2


def _router_kernel(u_ref, rw_ref, g_ref, mt_ref, cnt_ref):
    lt = _dot_nt(rw_ref[...], u_ref[...])
    e = lax.broadcasted_iota(I32, lt.shape, 0)
    top1 = jnp.max(lt, axis=0, keepdims=True)
    m1 = e == jnp.min(jnp.where(lt == top1, e, N_EXPERTS), axis=0, keepdims=True)
    l2 = jnp.where(m1, -3.0e38, lt)
    top2 = jnp.max(l2, axis=0, keepdims=True)
    m2 = e == jnp.min(jnp.where(l2 == top2, e, N_EXPERTS), axis=0, keepdims=True)
    e2 = jnp.exp(top2 - top1)
    den = 1.0 + e2
    gates = jnp.where(m1, 1.0 / den, 0.0) + jnp.where(m2, e2 / den, 0.0)
    member = jnp.where(m1 | m2, 1.0, 0.0)
    pad = jnp.zeros((LANE - 2 * N_EXPERTS, lt.shape[1]), F32)
    g_ref[...] = jnp.concatenate([gates, member, pad], axis=0).T
    mt_ref[...] = member
    cnt_ref[0] = jnp.broadcast_to(jnp.sum(member, axis=1, keepdims=True), (N_EXPERTS, LANE))


def moe_router(u2, router_w):
    t, d = u2.shape
    nt = t // MOE_TILE
    return pl.pallas_call(
        _router_kernel,
        grid=(nt,),
        in_specs=[pl.BlockSpec((MOE_TILE, d), lambda i: (i, 0)), pl.BlockSpec((N_EXPERTS, d), lambda i: (0, 0))],
        out_specs=[
            pl.BlockSpec((MOE_TILE, LANE), lambda i: (i, 0)),
            pl.BlockSpec((N_EXPERTS, MOE_TILE), lambda i: (0, i)),
            pl.BlockSpec((1, N_EXPERTS, LANE), lambda i: (i, 0, 0)),
        ],
        out_shape=[
            jax.ShapeDtypeStruct((t, LANE), F32),
            jax.ShapeDtypeStruct((N_EXPERTS, t), F32),
            jax.ShapeDtypeStruct((nt, N_EXPERTS, LANE), F32),
        ],
        compiler_params=_cparams(("parallel",)),
        name="moe_router",
    )(u2, router_w.T.astype(BF16))


def _window_dma(op, hbm_ref, buf_ref, sem_ref, base_ref, cnt_ref, step, slot, to_hbm):
    def copy(e, half):
        rows = pl.ds(pl.multiple_of(base_ref[step * N_EXPERTS + e] + half * MOE_HALF, MOE_ALIGN), MOE_HALF)
        vmem = buf_ref.at[slot, e, pl.ds(half * MOE_HALF, MOE_HALF)]
        src, dst = (vmem, hbm_ref.at[rows]) if to_hbm else (hbm_ref.at[rows], vmem)
        getattr(pltpu.make_async_copy(src, dst, sem_ref.at[slot]), op)()

    for e in range(N_EXPERTS):
        copy(e, 0)
        pl.when(cnt_ref[step * N_EXPERTS + e] > MOE_HALF)(functools.partial(copy, e, 1))


def _dispatch_kernel(base_ref, cnt_ref, u_ref, mt_ref, xs_in_ref, xs_ref, buf_ref, sem_ref):
    del xs_in_ref
    i = pl.program_id(0)
    slot = i % 2
    dma = functools.partial(_window_dma, hbm_ref=xs_ref, buf_ref=buf_ref, sem_ref=sem_ref, base_ref=base_ref,
                            cnt_ref=cnt_ref, to_hbm=True)
    mt = mt_ref[...]
    n = mt.shape[1]
    ri = lax.broadcasted_iota(I32, (n, n), 0)
    ci = lax.broadcasted_iota(I32, (n, n), 1)
    before = jnp.where(ri < ci, 1.0, 0.0).astype(BF16)
    rank = _dot(mt.astype(BF16), before).astype(I32)
    wr = lax.broadcasted_iota(I32, (MOE_HALF, n), 0)

    def pick(e, half):
        hit = (wr + half * MOE_HALF == rank[e:e + 1, :]) & (mt[e:e + 1, :] > 0.0)
        return jnp.where(hit, 1.0, 0.0).astype(BF16)

    low = _dot(jnp.concatenate([pick(e, 0) for e in range(N_EXPERTS)], axis=0), u_ref[...]).astype(BF16)
    for e in range(N_EXPERTS):
        buf_ref[slot, e, 0:MOE_HALF] = low[e * MOE_HALF:(e + 1) * MOE_HALF]

    def gather_upper(e):
        buf_ref[slot, e, MOE_HALF:MOE_WIN] = _dot(pick(e, 1), u_ref[...]).astype(BF16)

    for e in range(N_EXPERTS):
        pl.when(cnt_ref[i * N_EXPERTS + e] > MOE_HALF)(functools.partial(gather_upper, e))

    pl.when(i > 0)(lambda: dma("wait", step=i - 1, slot=1 - slot))
    dma("start", step=i, slot=slot)
    pl.when(i == pl.num_programs(0) - 1)(lambda: dma("wait", step=i, slot=slot))


def moe_dispatch(u2, member_t, base, cnt, n_rows):
    t, d = u2.shape
    nt = t // MOE_TILE
    xs0 = jnp.zeros((n_rows, d), BF16)
    grid_spec = pltpu.PrefetchScalarGridSpec(
        num_scalar_prefetch=2,
        grid=(nt,),
        in_specs=[
            pl.BlockSpec((MOE_TILE, d), lambda i, base, cnt: (i, 0)),
            pl.BlockSpec((N_EXPERTS, MOE_TILE), lambda i, base, cnt: (0, i)),
            pl.BlockSpec(memory_space=pl.ANY),
        ],
        out_specs=pl.BlockSpec(memory_space=pl.ANY),
        scratch_shapes=[
            pltpu.VMEM((2, N_EXPERTS, MOE_WIN, d), BF16),
            pltpu.SemaphoreType.DMA((2,)),
        ],
    )
    return pl.pallas_call(
        _dispatch_kernel,
        grid_spec=grid_spec,
        out_shape=jax.ShapeDtypeStruct((n_rows, d), BF16),
        input_output_aliases={4: 0},
        compiler_params=_cparams(("arbitrary",)),
        name="moe_dispatch",
    )(base.reshape(-1), cnt.reshape(-1), u2, member_t, xs0)


def _expert_kernel(be_ref, act_ref, x_ref, wg_ref, wu_ref, wd_ref, y_ref, acc_ref):
    i = pl.program_id(0)

    @pl.when(act_ref[i] > 0)
    def _():
        x = x_ref[...]
        for k, (f0, fc) in enumerate(_chunks(wg_ref.shape[2], FCHUNK)):
            hid = (_silu(_dot(x, wg_ref[0, :, f0:f0 + fc])) * _dot(x, wu_ref[0, :, f0:f0 + fc])).astype(BF16)
            part = _dot(hid, wd_ref[0, f0:f0 + fc, :])
            if k == 0:
                acc_ref[...] = part
            else:
                acc_ref[...] += part
        y_ref[...] = acc_ref[...].astype(BF16)

    @pl.when(act_ref[i] == 0)
    def _():
        y_ref[...] = jnp.zeros_like(y_ref)


def moe_experts(xs, blk_expert, blk_active, wg, wu, wd):
    n_rows, d = xs.shape
    f = wg.shape[2]
    resident = dict(pipeline_mode=pl.Buffered(1))
    grid_spec = pltpu.PrefetchScalarGridSpec(
        num_scalar_prefetch=2,
        grid=(n_rows // MOE_BLK,),
        in_specs=[
            pl.BlockSpec((MOE_BLK, d), lambda i, be, act: (i, 0)),
            pl.BlockSpec((1, d, f), lambda i, be, act: (be[i], 0, 0), **resident),
            pl.BlockSpec((1, d, f), lambda i, be, act: (be[i], 0, 0), **resident),
            pl.BlockSpec((1, f, d), lambda i, be, act: (be[i], 0, 0), **resident),
        ],
        out_specs=pl.BlockSpec((MOE_BLK, d), lambda i, be, act: (i, 0)),
        scratch_shapes=[pltpu.VMEM((MOE_BLK, d), F32)],
    )
    return pl.pallas_call(
        _expert_kernel,
        grid_spec=grid_spec,
        out_shape=jax.ShapeDtypeStruct((n_rows, d), BF16),
        compiler_params=_cparams(("arbitrary",)),
        name="moe_experts",
    )(blk_expert, blk_active, xs, wg.astype(BF16), wu.astype(BF16), wd.astype(BF16))


def _combine_kernel(base_ref, cnt_ref, g_ref, x_ref, mod_ref, lnw_ref, lnb_ref, y_ref, o_ref,
                    buf_ref, sem_ref, f_ref):
    i = pl.program_id(0)
    slot = i % 2
    dma = functools.partial(_window_dma, hbm_ref=y_ref, buf_ref=buf_ref, sem_ref=sem_ref, base_ref=base_ref,
                            cnt_ref=cnt_ref, to_hbm=False)
    pl.when(i == 0)(lambda: dma("start", step=0, slot=0))
    pl.when(i + 1 < pl.num_programs(0))(lambda: dma("start", step=i + 1, slot=1 - slot))

    gm = g_ref[...]
    n = gm.shape[0]
    ri = lax.broadcasted_iota(I32, (n, n), 0)
    ci = lax.broadcasted_iota(I32, (n, n), 1)
    before = jnp.where(ci < ri, 1.0, 0.0).astype(BF16)
    lane = lax.broadcasted_iota(I32, (1, LANE), 1)
    member = jnp.where((lane >= N_EXPERTS) & (lane < 2 * N_EXPERTS), gm, 0.0)
    rank = _dot(before, member.astype(BF16)).astype(I32)
    wc = lax.broadcasted_iota(I32, (n, MOE_HALF), 1)
    dma("wait", step=i, slot=slot)

    def gathered(e, half):
        le = N_EXPERTS + e
        hit = (wc + half * MOE_HALF == rank[:, le:le + 1]) & (gm[:, le:le + 1] > 0.0)
        rows = buf_ref[slot, e, half * MOE_HALF:(half + 1) * MOE_HALF]
        return gm[:, e:e + 1] * _dot(jnp.where(hit, 1.0, 0.0).astype(BF16), rows)

    f = gathered(0, 0)
    for e in range(1, N_EXPERTS):
        f = f + gathered(e, 0)
    f_ref[...] = f

    def add_upper(e):
        f_ref[...] += gathered(e, 1)

    for e in range(N_EXPERTS):
        pl.when(cnt_ref[i * N_EXPERTS + e] > MOE_HALF)(functools.partial(add_upper, e))
    gate2 = mod_ref[0, 5:6, :]
    o_ref[...] = _ln_rows(DN_ALPHA * x_ref[...] + (1.0 + gate2) * f_ref[...]) * lnw_ref[...] + lnb_ref[...]


def moe_combine(y, gates, base, cnt, x1, mod, ln_w, ln_b, seq):
    t, d = x1.shape
    nt = t // MOE_TILE
    tpb = seq // MOE_TILE
    grid_spec = pltpu.PrefetchScalarGridSpec(
        num_scalar_prefetch=2,
        grid=(nt,),
        in_specs=[
            pl.BlockSpec((MOE_TILE, LANE), lambda i, base, cnt: (i, 0)),
            pl.BlockSpec((MOE_TILE, d), lambda i, base, cnt: (i, 0)),
            pl.BlockSpec((1, 6, d), lambda i, base, cnt: (i // tpb, 0, 0)),
            pl.BlockSpec((1, d), lambda i, base, cnt: (0, 0)),
            pl.BlockSpec((1, d), lambda i, base, cnt: (0, 0)),
            pl.BlockSpec(memory_space=pl.ANY),
        ],
        out_specs=pl.BlockSpec((MOE_TILE, d), lambda i, base, cnt: (i, 0)),
        scratch_shapes=[
            pltpu.VMEM((2, N_EXPERTS, MOE_WIN, d), BF16),
            pltpu.SemaphoreType.DMA((2,)),
            pltpu.VMEM((MOE_TILE, d), F32),
        ],
    )
    return pl.pallas_call(
        _combine_kernel,
        grid_spec=grid_spec,
        out_shape=jax.ShapeDtypeStruct((t, d), F32),
        compiler_params=_cparams(("arbitrary",)),
        name="moe_combine",
    )(base.reshape(-1), cnt.reshape(-1), gates, x1, mod, ln_w[None, :], ln_b[None, :], y)


def moe_ffn(u2, x1, mod, router_w, wg, wu, wd, ln_w, ln_b, seq):
    t, d = u2.shape
    nt = t // MOE_TILE
    gates, member_t, cnt = moe_router(u2, router_w)
    cnt = cnt[:, :, 0].astype(I32)
    seg = (cnt + MOE_ALIGN - 1) // MOE_ALIGN * MOE_ALIGN
    region = (jnp.sum(seg, axis=0) + MOE_HALF + MOE_BLK - 1) // MOE_BLK * MOE_BLK
    region_end = jnp.cumsum(region)
    base = ((region_end - region)[None, :] + jnp.cumsum(seg, axis=0) - seg).astype(I32)
    n_blk = -(-(2 * t + nt * N_EXPERTS * (MOE_ALIGN - 1) + N_EXPERTS * (MOE_HALF + MOE_BLK - 1)) // MOE_BLK)
    blk_start = jnp.arange(n_blk, dtype=I32) * MOE_BLK
    blk_expert = jnp.minimum(jnp.sum((blk_start[:, None] >= region_end[None, :]).astype(I32), axis=1), N_EXPERTS - 1)
    blk_active = (blk_start < region_end[-1]).astype(I32)
    xs = moe_dispatch(u2, member_t, base, cnt, n_blk * MOE_BLK)
    y = moe_experts(xs, blk_expert, blk_active, wg, wu, wd)
    return moe_combine(y, gates, base, cnt, x1, mod, ln_w, ln_b, seq)


def _mixer_front(x, mod, w_in, dt_bias, cmp_pos_k, cmp_w1_k, cmp_w2_k, cmp_pos_v, cmp_w1_v, cmp_w2_v):
    b, s, d = x.shape
    t = b * s
    g = N_KV_HEADS
    q, kv, z, xbc, sm = in_proj(x.reshape(t, d), mod, w_in, dt_bias, s)
    kv = kv.reshape(b, s, 6, 128)
    kvc = compress(jnp.stack([kv[:, :, 0], kv[:, :, 1]]),
                   jnp.stack([cmp_pos_k, cmp_pos_v]), jnp.stack([cmp_w1_k, cmp_w1_v]),
                   jnp.stack([cmp_w2_k, cmp_w2_v]))
    heads = lambda a: a.reshape(b, -1, g, HEAD_DIM).transpose(0, 2, 1, 3)
    gates = sm[:, :GATE_COLS].reshape(b, s, g, 3 * ATT_GQA).transpose(0, 2, 3, 1)
    att = nsa_attention(q.reshape(b, s, ATT_WIDTH), gates, heads(kvc[0]), heads(kvc[1]),
                        heads(kv[:, :, 2]), heads(kv[:, :, 3]), heads(kv[:, :, 4]), heads(kv[:, :, 5]))
    return att, z, xbc, sm


def kernel(x, c, ada_w, ada_b, w_in, cmp_pos_k, cmp_w1_k, cmp_w2_k, cmp_pos_v, cmp_w1_v, cmp_w2_v, attn_norm_w, conv_w, conv_b, dt_bias, a_log, d_skip, ssm_norm_w, w_out, ln1_w, ln1_b, ln2_w, ln2_b, ffn_w_gate, ffn_w_up, ffn_w_down, router_w, exp_w_gate, exp_w_up, exp_w_down):
    b, s, d = x.shape
    t = b * s
    mod = ada_mod(c, ada_w, ada_b).reshape(DEPTH, b, 6, d)
    xc = x.reshape(t, d)
    for l in range(DEPTH):
        att, z, xbc, sm = _mixer_front(xc.reshape(b, s, d), mod[l], w_in[l], dt_bias[l],
                                       cmp_pos_k[l], cmp_w1_k[l], cmp_w2_k[l],
                                       cmp_pos_v[l], cmp_w1_v[l], cmp_w2_v[l])
        ssm = ssd_mixer(xbc, z, sm, conv_w[l], conv_b[l], a_log[l], d_skip[l], ssm_norm_w[l], b, s)
        x1, u2 = out_proj(att.reshape(t, ATT_WIDTH), ssm.reshape(t, SSM_WIDTH), xc, mod[l], attn_norm_w[l],
                          w_out[l], ln1_w[l], ln1_b[l], s)
        i = l // 2
        if l % 2 == 0:
            xc = dense_ffn(u2, x1, mod[l], ffn_w_gate[i], ffn_w_up[i], ffn_w_down[i], ln2_w[l], ln2_b[l], s)
        else:
            xc = moe_ffn(u2, x1, mod[l], router_w[i], exp_w_gate[i], exp_w_up[i], exp_w_down[i],
                         ln2_w[l], ln2_b[l], s)
    return xc.reshape(b, s, d)
```
